```python
import jax, jax.numpy as jnp
from jax import lax
import numpy as np

D_MODEL = 2048
BATCH = 4
SEQ = 2048
DEPTH = 2
DEC_BATCH = 128
DEC_SEQ = 1
PAST_LEN = 16384
PAGE_SIZE = 128

WIDTH_A = D_MODEL // 2
WIDTH_B = D_MODEL - WIDTH_A
GROUPS_A = 8
GROUPS_B = 8
CONV_A_W = 31
CONV_B_W = 3
PROJ_IN = 2 * WIDTH_A + 3 * WIDTH_B
PEER_HEADS = 8
N_KEYS = 128
N_EXPERTS = N_KEYS * N_KEYS
PEER_QDIM = 256
HALF = PEER_QDIM // 2
TOPK = 16
PEER_BLOCK = 128
PLE_DIM = 256
RMS_EPS = 1e-6
LN_EPS = 1e-5

kernel_name = "hybrid_conformerconv_shortconv_peer_decode_step"


def rmsnorm(x, g):
    xf = x.astype(jnp.float32)
    y = xf * lax.rsqrt(jnp.mean(xf * xf, axis=-1, keepdims=True) + RMS_EPS)
    return (y * g.astype(jnp.float32)).astype(x.dtype)


def layernorm(x, g, b):
    xf = x.astype(jnp.float32)
    mu = jnp.mean(xf, axis=-1, keepdims=True)
    xc = xf - mu
    var = jnp.mean(xc * xc, axis=-1, keepdims=True)
    y = xc * lax.rsqrt(var + LN_EPS) * g.astype(jnp.float32) + b.astype(jnp.float32)
    return y.astype(x.dtype)


def causal_dwconv(prev, x, w):
    width = w.shape[0]
    full = jnp.concatenate([prev.astype(x.dtype), x], axis=1)
    y = lax.conv_general_dilated(
        full, w[:, None, :].astype(x.dtype), window_strides=(1,), padding='VALID',
        dimension_numbers=('NWC', 'WIO', 'NWC'), feature_group_count=x.shape[-1])
    return y, full[:, full.shape[1] - (width - 1):, :]


def token_mixer(u, prev_a, prev_b, w_in, conv_a_w, conv_a_b, ln_a_g, ln_a_b, conv_b_w, w_out):
    proj = jnp.einsum('btd,de->bte', u, w_in)
    a_val, a_gate, b_gate, c_gate, b_in = jnp.split(
        proj, [WIDTH_A, 2 * WIDTH_A, 2 * WIDTH_A + WIDTH_B, 2 * WIDTH_A + 2 * WIDTH_B], axis=-1)
    a_glu = a_val * jax.nn.sigmoid(a_gate)
    a, new_a = causal_dwconv(prev_a, a_glu, conv_a_w)
    a = jax.nn.silu(layernorm(a + conv_a_b.astype(a.dtype), ln_a_g, ln_a_b))
    z, new_b = causal_dwconv(prev_b, c_gate * b_in, conv_b_w)
    yb = b_gate * z
    y = jnp.einsum('bte,ed->btd', jnp.concatenate([a, yb], axis=-1), w_out)
    return y, new_a, new_b


def peer(u, w_q, k1, k2, e_u, e_v):
    bsz, t, d = u.shape
    x = u.reshape(bsz * t, d)
    n = x.shape[0]
    pad = (-n) % PEER_BLOCK
    blocks = jnp.pad(x, ((0, pad), (0, 0))).reshape(-1, PEER_BLOCK, d)

    def one_block(xb):
        q = jnp.einsum('td,de->te', xb, w_q).reshape(PEER_BLOCK, PEER_HEADS, 2, HALF)
        s1 = jnp.einsum('thc,kc->thk', q[:, :, 0], k1).astype(jnp.float32)
        s2 = jnp.einsum('thc,kc->thk', q[:, :, 1], k2).astype(jnp.float32)
        v1, i1 = lax.top_k(s1, TOPK)
        v2, i2 = lax.top_k(s2, TOPK)
        cand = (v1[..., :, None] + v2[..., None, :]).reshape(PEER_BLOCK, PEER_HEADS, TOPK * TOPK)
        cidx = (i1[..., :, None] * N_KEYS + i2[..., None, :]).reshape(PEER_BLOCK, PEER_HEADS, TOPK * TOPK)
        sc, sel = lax.top_k(cand, TOPK)
        idx = jnp.take_along_axis(cidx, sel, axis=-1)
        g = jax.nn.softmax(sc, axis=-1).astype(xb.dtype)
        act = jax.nn.gelu(jnp.einsum('thkd,td->thk', e_u[idx], xb), approximate=False)
        return jnp.einsum('thk,thkd->td', g * act, e_v[idx])

    out = lax.map(one_block, blocks).reshape(-1, d)[:n]
    return out.reshape(bsz, t, d)


def trunk(x, p, prev_a, prev_b, norm1_g, w_in, conv_a_w, conv_a_b, ln_a_g, ln_a_b, conv_b_w,
          w_out, norm2_g, peer_wq, peer_k1, peer_k2, peer_u, peer_v, norm3_g, ple_w,
          ple_gate_w, final_g):
    h = x
    new_as, new_bs = [], []
    for i in range(DEPTH):
        mix, na, nb = token_mixer(rmsnorm(h, norm1_g[i]), prev_a[i], prev_b[i], w_in[i],
                                  conv_a_w[i], conv_a_b[i], ln_a_g[i], ln_a_b[i], conv_b_w[i], w_out[i])
        h = h + mix
        h = h + peer(rmsnorm(h, norm2_g[i]), peer_wq[i], peer_k1[i], peer_k2[i], peer_u[i], peer_v[i])
        gate = jax.nn.sigmoid(jnp.einsum('btd,de->bte', rmsnorm(h, norm3_g[i]), ple_gate_w[i]))
        h = h + jnp.einsum('btc,cd->btd', p[i].astype(h.dtype), ple_w[i]) * gate
        new_as.append(na)
        new_bs.append(nb)
    return rmsnorm(h, final_g), jnp.stack(new_as), jnp.stack(new_bs)


def setup_inputs(seed: int = 0) -> dict:
    key = jax.random.key(seed)
    ks = jax.random.split(key, 24)
    f32 = jnp.float32
    nrm = lambda k, s, sc: jax.random.normal(k, s, f32) * sc
    return {
        "x_prompt": nrm(ks[0], (BATCH, SEQ, D_MODEL), 1.0),
        "x_sample": nrm(ks[1], (DEC_BATCH, DEC_SEQ, D_MODEL), 1.0),
        "p_prompt": nrm(ks[2], (DEPTH, BATCH, SEQ, PLE_DIM), 1.0),
        "p_sample": nrm(ks[3], (DEPTH, DEC_BATCH, DEC_SEQ, PLE_DIM), 1.0),
        "state_conv_a": nrm(ks[4], (DEPTH, DEC_BATCH, CONV_A_W - 1, WIDTH_A), 0.5),
        "state_conv_b": nrm(ks[5], (DEPTH, DEC_BATCH, CONV_B_W - 1, WIDTH_B), 0.5),
        "norm1_g": 1.0 + nrm(ks[6], (DEPTH, D_MODEL), 0.02),
        "w_in": nrm(ks[7], (DEPTH, D_MODEL, PROJ_IN), D_MODEL ** -0.5),
        "conv_a_w": nrm(ks[8], (DEPTH, CONV_A_W, WIDTH_A), CONV_A_W ** -0.5),
        "conv_a_b": nrm(ks[9], (DEPTH, WIDTH_A), 0.02),
        "ln_a_g": 1.0 + nrm(ks[10], (DEPTH, WIDTH_A), 0.02),
        "ln_a_b": nrm(ks[11], (DEPTH, WIDTH_A), 0.02),
        "conv_b_w": nrm(ks[12], (DEPTH, CONV_B_W, WIDTH_B), CONV_B_W ** -0.5),
        "w_out": nrm(ks[13], (DEPTH, WIDTH_A + WIDTH_B, D_MODEL), (WIDTH_A + WIDTH_B) ** -0.5),
        "norm2_g": 1.0 + nrm(ks[14], (DEPTH, D_MODEL), 0.02),
        "peer_wq": nrm(ks[15], (DEPTH, D_MODEL, PEER_HEADS * PEER_QDIM), D_MODEL ** -0.5),
        "peer_k1": nrm(ks[16], (DEPTH, N_KEYS, HALF), HALF ** -0.5),
        "peer_k2": nrm(ks[17], (DEPTH, N_KEYS, HALF), HALF ** -0.5),
        "peer_u": nrm(ks[18], (DEPTH, N_EXPERTS, D_MODEL), D_MODEL ** -0.5),
        "peer_v": nrm(ks[19], (DEPTH, N_EXPERTS, D_MODEL), PEER_HEADS ** -0.5),
        "norm3_g": 1.0 + nrm(ks[20], (DEPTH, D_MODEL), 0.02),
        "ple_w": nrm(ks[21], (DEPTH, PLE_DIM, D_MODEL), PLE_DIM ** -0.5),
        "ple_gate_w": nrm(ks[22], (DEPTH, D_MODEL, D_MODEL), D_MODEL ** -0.5),
        "final_g": 1.0 + nrm(ks[23], (D_MODEL,), 0.02),
    }


def reference(x_prompt, x_sample, p_prompt, p_sample, state_conv_a, state_conv_b,
              norm1_g, w_in, conv_a_w, conv_a_b, ln_a_g, ln_a_b, conv_b_w, w_out,
              norm2_g, peer_wq, peer_k1, peer_k2, peer_u, peer_v, norm3_g, ple_w,
              ple_gate_w, final_g):
    weights = (norm1_g, w_in, conv_a_w, conv_a_b, ln_a_g, ln_a_b, conv_b_w, w_out,
               norm2_g, peer_wq, peer_k1, peer_k2, peer_u, peer_v, norm3_g, ple_w,
               ple_gate_w, final_g)
    zero_a = jnp.zeros((DEPTH, x_prompt.shape[0], CONV_A_W - 1, WIDTH_A), x_prompt.dtype)
    zero_b = jnp.zeros((DEPTH, x_prompt.shape[0], CONV_B_W - 1, WIDTH_B), x_prompt.dtype)
    y_prompt, conv_a_prompt, conv_b_prompt = trunk(x_prompt, p_prompt, zero_a, zero_b, *weights)
    y_sample, conv_a_sample, conv_b_sample = trunk(x_sample, p_sample, state_conv_a, state_conv_b, *weights)
    return (y_prompt, y_sample, conv_a_prompt, conv_a_sample, conv_b_prompt, conv_b_sample)
```

```python
import functools
import math

import jax
import jax.numpy as jnp
from jax import lax
from jax.experimental import pallas as pl
from jax.experimental.pallas import tpu as pltpu

D_MODEL = 2048
WIDTH = D_MODEL // 2
CONV_A_W = 31
CONV_B_W = 3
PROJ_IN = 5 * WIDTH
PEER_HEADS = 8
N_KEYS = 128
N_EXPERTS = N_KEYS * N_KEYS
HALF = 128
PEER_QDIM = 2 * HALF
TOPK = 16
RMS_EPS = 1e-6
LN_EPS = 1e-5

LANES = 128
SUBLANES = 8
A_HALO = 32
B_HALO = 8
VMEM_LIMIT = 56 * 1024 * 1024

_NT = (((1,), (1,)), ((), ()))
_TN = (((0,), (0,)), ((), ()))


def _params(sem):
    return pltpu.CompilerParams(dimension_semantics=sem, vmem_limit_bytes=VMEM_LIMIT)


def _token_tile(n):
    for t in (640, 512, 256, 128):
        if n % t == 0:
            return t
    raise ValueError(f"token count {n} must be a multiple of {LANES}")


def _rms(x, g):
    ms = jnp.mean(x * x, axis=-1, keepdims=True)
    return x * lax.rsqrt(ms + RMS_EPS) * g


def _sigmoid(x):
    return 1.0 / (1.0 + jnp.exp(-x))


def _norm_matmul_kernel(x_ref, g_ref, w_ref, o_ref, u_scr):
    @pl.when(pl.program_id(1) == 0)
    def _():
        u_scr[...] = _rms(x_ref[...], g_ref[...]).astype(jnp.bfloat16)

    o_ref[...] = jnp.dot(u_scr[...], w_ref[...], preferred_element_type=jnp.float32)


def _norm_matmul(x, g, w, layer, tm, tn):
    n, d = x.shape
    e = w.shape[-1]
    return pl.pallas_call(
        _norm_matmul_kernel,
        grid=(n // tm, e // tn),
        in_specs=[
            pl.BlockSpec((tm, d), lambda i, j: (i, 0)),
            pl.BlockSpec((None, 1, d), lambda i, j: (layer, 0, 0)),
            pl.BlockSpec((None, d, tn), lambda i, j: (layer, 0, j)),
        ],
        out_specs=pl.BlockSpec((tm, tn), lambda i, j: (i, j)),
        out_shape=jax.ShapeDtypeStruct((n, e), jnp.float32),
        scratch_shapes=[pltpu.VMEM((tm, d), jnp.bfloat16)],
        compiler_params=_params(("parallel", "arbitrary")),
        name="norm_matmul",
    )(x, g, w)


def _ln_silu(y, g, b):
    mu = jnp.mean(y, axis=-1, keepdims=True)
    yc = y - mu
    var = jnp.mean(yc * yc, axis=-1, keepdims=True)
    ln = yc * lax.rsqrt(var + LN_EPS) * g + b
    return ln * _sigmoid(ln)


def _conv_prompt_kernel(p_ref, wa_ref, ba_ref, lg_ref, lb_ref, wb_ref,
                        mix_ref, na_ref, nb_ref, abuf, bbuf, ybuf, *, tb, cw):
    t = pl.program_id(1)

    @pl.when(t == 0)
    def _():
        abuf[0:A_HALO, :] = jnp.zeros((A_HALO, WIDTH), jnp.float32)
        bbuf[0:B_HALO, :] = jnp.zeros((B_HALO, WIDTH), jnp.float32)

    @pl.when(t > 0)
    def _():
        abuf[0:A_HALO, :] = abuf[tb:tb + A_HALO, :]
        bbuf[0:B_HALO, :] = bbuf[tb:tb + B_HALO, :]

    a_glu = p_ref[:, 0:WIDTH] * _sigmoid(p_ref[:, WIDTH:2 * WIDTH])
    abuf[A_HALO:A_HALO + tb, :] = a_glu
    off_a = A_HALO - (CONV_A_W - 1)
    for c0 in range(0, WIDTH, cw):
        acc = jnp.zeros((tb, cw), jnp.float32)
        for j in range(CONV_A_W):
            acc = acc + wa_ref[j:j + 1, c0:c0 + cw] * abuf[off_a + j:off_a + j + tb, c0:c0 + cw]
        ybuf[:, c0:c0 + cw] = acc + ba_ref[:, c0:c0 + cw]
    a = _ln_silu(ybuf[...], lg_ref[...], lb_ref[...])
    mix_ref[:, 0:WIDTH] = a.astype(mix_ref.dtype)

    cb = p_ref[:, 3 * WIDTH:4 * WIDTH] * p_ref[:, 4 * WIDTH:5 * WIDTH]
    bbuf[B_HALO:B_HALO + tb, :] = cb
    off_b = B_HALO - (CONV_B_W - 1)
    z = jnp.zeros((tb, WIDTH), jnp.float32)
    for j in range(CONV_B_W):
        z = z + wb_ref[j:j + 1, :] * bbuf[off_b + j:off_b + j + tb, :]
    mix_ref[:, WIDTH:2 * WIDTH] = (p_ref[:, 2 * WIDTH:3 * WIDTH] * z).astype(mix_ref.dtype)

    na_ref[...] = abuf[tb:tb + A_HALO, :]
    nb_ref[...] = bbuf[tb:tb + B_HALO, :]


def _conv_prompt(proj, wa, ba, lg, lb, wb, layer, batch, seq, tb):
    n = proj.shape[0]
    nt = seq // tb
    vec = lambda: pl.BlockSpec((None, 1, WIDTH), lambda b, t: (layer, 0, 0))
    return pl.pallas_call(
        functools.partial(_conv_prompt_kernel, tb=tb, cw=2 * LANES),
        grid=(batch, nt),
        in_specs=[
            pl.BlockSpec((tb, PROJ_IN), lambda b, t: (b * nt + t, 0)),
            pl.BlockSpec((None, CONV_A_W, WIDTH), lambda b, t: (layer, 0, 0)),
            vec(), vec(), vec(),
            pl.BlockSpec((None, CONV_B_W, WIDTH), lambda b, t: (layer, 0, 0)),
        ],
        out_specs=[
            pl.BlockSpec((tb, D_MODEL), lambda b, t: (b * nt + t, 0)),
            pl.BlockSpec((None, A_HALO, WIDTH), lambda b, t: (b, 0, 0)),
            pl.BlockSpec((None, B_HALO, WIDTH), lambda b, t: (b, 0, 0)),
        ],
        out_shape=[
            jax.ShapeDtypeStruct((n, D_MODEL), jnp.bfloat16),
            jax.ShapeDtypeStruct((batch, A_HALO, WIDTH), jnp.float32),
            jax.ShapeDtypeStruct((batch, B_HALO, WIDTH), jnp.float32),
        ],
        scratch_shapes=[
            pltpu.VMEM((A_HALO + tb, WIDTH), jnp.float32),
            pltpu.VMEM((B_HALO + tb, WIDTH), jnp.float32),
            pltpu.VMEM((tb, WIDTH), jnp.float32),
        ],
        compiler_params=_params(("parallel", "arbitrary")),
        name="conv_prompt",
    )(proj, wa, ba, lg, lb, wb)


def _conv_sample_kernel(p_ref, sa_ref, sb_ref, wa_ref, ba_ref, lg_ref, lb_ref, wb_ref, mix_in,
                        mix_ref, na_ref, nb_ref):
    del mix_in
    ha = CONV_A_W - 1
    hb = CONV_B_W - 1
    a_glu = p_ref[:, 0:WIDTH] * _sigmoid(p_ref[:, WIDTH:2 * WIDTH])
    acc = wa_ref[ha:ha + 1, :] * a_glu
    for j in range(ha):
        acc = acc + wa_ref[j:j + 1, :] * sa_ref[:, j * WIDTH:(j + 1) * WIDTH]
    a = _ln_silu(acc + ba_ref[...], lg_ref[...], lb_ref[...])
    mix_ref[:, 0:WIDTH] = a.astype(mix_ref.dtype)
    na_ref[:, 0:(ha - 1) * WIDTH] = sa_ref[:, WIDTH:ha * WIDTH]
    na_ref[:, (ha - 1) * WIDTH:ha * WIDTH] = a_glu

    cb = p_ref[:, 3 * WIDTH:4 * WIDTH] * p_ref[:, 4 * WIDTH:5 * WIDTH]
    z = wb_ref[hb:hb + 1, :] * cb
    for j in range(hb):
        z = z + wb_ref[j:j + 1, :] * sb_ref[:, j * WIDTH:(j + 1) * WIDTH]
    mix_ref[:, WIDTH:2 * WIDTH] = (p_ref[:, 2 * WIDTH:3 * WIDTH] * z).astype(mix_ref.dtype)
    nb_ref[:, 0:(hb - 1) * WIDTH] = sb_ref[:, WIDTH:hb * WIDTH]
    nb_ref[:, (hb - 1) * WIDTH:hb * WIDTH] = cb


def _conv_sample(proj, sa2, sb2, wa, ba, lg, lb, wb, mix, layer, n_prompt, bc):
    dec = sa2.shape[1]
    ha = CONV_A_W - 1
    hb = CONV_B_W - 1
    r0 = n_prompt // bc
    vec = lambda: pl.BlockSpec((None, 1, WIDTH), lambda i: (layer, 0, 0))
    return pl.pallas_call(
        _conv_sample_kernel,
        grid=(dec // bc,),
        in_specs=[
            pl.BlockSpec((bc, PROJ_IN), lambda i: (r0 + i, 0)),
            pl.BlockSpec((None, bc, ha * WIDTH), lambda i: (layer, i, 0)),
            pl.BlockSpec((None, bc, hb * WIDTH), lambda i: (layer, i, 0)),
            pl.BlockSpec((None, CONV_A_W, WIDTH), lambda i: (layer, 0, 0)),
            vec(), vec(), vec(),
            pl.BlockSpec((None, CONV_B_W, WIDTH), lambda i: (layer, 0, 0)),
            pl.BlockSpec(memory_space=pl.ANY),
        ],
        out_specs=[
            pl.BlockSpec((bc, D_MODEL), lambda i: (r0 + i, 0)),
            pl.BlockSpec((bc, ha * WIDTH), lambda i: (i, 0)),
            pl.BlockSpec((bc, hb * WIDTH), lambda i: (i, 0)),
        ],
        out_shape=[
            jax.ShapeDtypeStruct(mix.shape, mix.dtype),
            jax.ShapeDtypeStruct((dec, ha * WIDTH), jnp.float32),
            jax.ShapeDtypeStruct((dec, hb * WIDTH), jnp.float32),
        ],
        input_output_aliases={8: 0},
        compiler_params=_params(("parallel",)),
        name="conv_sample",
    )(proj, sa2, sb2, wa, ba, lg, lb, wb, mix)


def _matmul_res_kernel(m_ref, w_ref, h_ref, o_ref):
    o_ref[...] = h_ref[...] + jnp.dot(m_ref[...], w_ref[...],
                                      preferred_element_type=jnp.float32)


def _matmul_res(mix, w, h, layer, tm, tn):
    n, k = mix.shape
    d = w.shape[-1]
    return pl.pallas_call(
        _matmul_res_kernel,
        grid=(n // tm, d // tn),
        in_specs=[
            pl.BlockSpec((tm, k), lambda i, j: (i, 0)),
            pl.BlockSpec((None, k, tn), lambda i, j: (layer, 0, j)),
            pl.BlockSpec((tm, tn), lambda i, j: (i, j)),
        ],
        out_specs=pl.BlockSpec((tm, tn), lambda i, j: (i, j)),
        out_shape=jax.ShapeDtypeStruct((n, d), jnp.float32),
        compiler_params=_params(("parallel", "arbitrary")),
        name="matmul_res",
    )(mix, w, h)


def _peer_scores_kernel(h_ref, g_ref, wq_ref, k1_ref, k2_ref, u_ref, s1_ref, s2_ref, q_scr):
    u = _rms(h_ref[...], g_ref[...]).astype(jnp.bfloat16)
    u_ref[...] = u
    q_scr[...] = jnp.dot(u, wq_ref[...], preferred_element_type=jnp.float32)
    k1 = k1_ref[...].astype(jnp.bfloat16)
    k2 = k2_ref[...].astype(jnp.bfloat16)
    for h in range(PEER_HEADS):
        c0 = h * PEER_QDIM
        q1 = q_scr[:, c0:c0 + HALF].astype(jnp.bfloat16)
        q2 = q_scr[:, c0 + HALF:c0 + PEER_QDIM].astype(jnp.bfloat16)
        s1_ref[h] = lax.dot_general(k1, q1, _NT, preferred_element_type=jnp.float32)
        s2_ref[h] = lax.dot_general(k2, q2, _NT, preferred_element_type=jnp.float32)


def _peer_scores(h, g, wq, k1, k2, layer, tm):
    n, d = h.shape
    qd = wq.shape[-1]
    sshape = jax.ShapeDtypeStruct((PEER_HEADS, N_KEYS, n), jnp.float32)
    sspec = lambda: pl.BlockSpec((PEER_HEADS, N_KEYS, tm), lambda i: (0, 0, i))
    return pl.pallas_call(
        _peer_scores_kernel,
        grid=(n // tm,),
        in_specs=[
            pl.BlockSpec((tm, d), lambda i: (i, 0)),
            pl.BlockSpec((None, 1, d), lambda i: (layer, 0, 0)),
            pl.BlockSpec((None, d, qd), lambda i: (layer, 0, 0)),
            pl.BlockSpec((None, N_KEYS, HALF), lambda i: (layer, 0, 0)),
            pl.BlockSpec((None, N_KEYS, HALF), lambda i: (layer, 0, 0)),
        ],
        out_specs=[pl.BlockSpec((tm, d), lambda i: (i, 0)), sspec(), sspec()],
        out_shape=[jax.ShapeDtypeStruct((n, d), jnp.bfloat16), sshape, sshape],
        scratch_shapes=[pltpu.VMEM((tm, qd), jnp.float32)],
        compiler_params=_params(("parallel",)),
        name="peer_scores",
    )(h, g, wq, k1, k2)


def _top_values(x, k):
    out = []
    for _ in range(k):
        m = jnp.max(x, axis=0, keepdims=True)
        out.append(m)
        x = jnp.where(x == m, -jnp.inf, x)
    return out


def _peer_topk_kernel(s1_ref, s2_ref, tau_ref, e1_ref, e2_ref):
    def head(h, carry):
        s1 = s1_ref[h]
        s2 = s2_ref[h]
        v1 = _top_values(s1, TOPK)
        v2 = jnp.concatenate(_top_values(s2, TOPK), axis=0)
        cand = jnp.concatenate([v + v2 for v in v1], axis=0)
        sc = _top_values(cand, TOPK)
        top = sc[0]
        z = jnp.zeros_like(top)
        for c in sc:
            z = z + jnp.exp(c - top)
        tau_ref[pl.ds(h, 1), :] = sc[TOPK - 1]
        e1_ref[h] = jnp.exp(s1 - v1[0]) * (1.0 / z)
        e2_ref[h] = jnp.exp(s2 - v2[0:1, :])
        return carry

    lax.fori_loop(0, PEER_HEADS, head, 0)


def _peer_topk(s1, s2):
    n = s1.shape[-1]
    sspec = lambda: pl.BlockSpec((PEER_HEADS, N_KEYS, LANES), lambda i: (0, 0, i))
    return pl.pallas_call(
        _peer_topk_kernel,
        grid=(n // LANES,),
        in_specs=[sspec(), sspec()],
        out_specs=[pl.BlockSpec((PEER_HEADS, LANES), lambda i: (0, i)), sspec(), sspec()],
        out_shape=[
            jax.ShapeDtypeStruct((PEER_HEADS, n), jnp.float32),
            jax.ShapeDtypeStruct(s1.shape, jnp.float32),
            jax.ShapeDtypeStruct(s1.shape, jnp.float32),
        ],
        compiler_params=_params(("parallel",)),
        name="peer_topk",
    )(s1, s2)


def _peer_kernel(u_ref, s1_ref, s2_ref, e1_ref, e2_ref, tau_ref, eu_ref, ev_ref, o_ref,
                 a_scr, w_scr, *, tm, te):
    j = pl.program_id(1)

    @pl.when(j == 0)
    def _():
        o_ref[...] = jnp.zeros_like(o_ref)

    a_scr[...] = lax.dot_general(eu_ref[...], u_ref[...], _NT,
                                 preferred_element_type=jnp.float32)
    n_i1 = te // N_KEYS
    i1_0 = pl.multiple_of(j * n_i1, SUBLANES)

    def lane_group(lg, carry):
        ls = pl.ds(pl.multiple_of(lg * LANES, LANES), LANES)
        for il in range(n_i1):
            g = jnp.zeros((N_KEYS, LANES), jnp.float32)
            for h in range(PEER_HEADS):
                s1_rows = s1_ref[h, pl.ds(i1_0, n_i1), ls]
                e1_rows = e1_ref[h, pl.ds(i1_0, n_i1), ls]
                c = s2_ref[h, :, ls] + s1_rows[il:il + 1, :]
                e = e2_ref[h, :, ls] * e1_rows[il:il + 1, :]
                g = g + jnp.where(c >= tau_ref[h:h + 1, ls], e, 0.0)
            a = a_scr[il * N_KEYS:(il + 1) * N_KEYS, ls]
            act = 0.5 * a * (1.0 + lax.erf(a * math.sqrt(0.5)))
            w_scr[il * N_KEYS:(il + 1) * N_KEYS, ls] = (g * act).astype(w_scr.dtype)
        return carry

    lax.fori_loop(0, tm // LANES, lane_group, 0)
    o_ref[...] += lax.dot_general(w_scr[...], ev_ref[...], _TN,
                                  preferred_element_type=jnp.float32)


def _peer(u, s1, s2, e1, e2, tau, eu, ev, layer, tm, te):
    n, d = u.shape
    assert (te // N_KEYS) % SUBLANES == 0
    once = pl.Buffered(1)
    sspec = lambda: pl.BlockSpec((PEER_HEADS, N_KEYS, tm), lambda i, j: (0, 0, i),
                                 pipeline_mode=once)
    return pl.pallas_call(
        functools.partial(_peer_kernel, tm=tm, te=te),
        grid=(n // tm, N_EXPERTS // te),
        in_specs=[
            pl.BlockSpec((tm, d), lambda i, j: (i, 0), pipeline_mode=once),
            sspec(), sspec(), sspec(), sspec(),
            pl.BlockSpec((PEER_HEADS, tm), lambda i, j: (0, i)),
            pl.BlockSpec((None, te, d), lambda i, j: (layer, j, 0)),
            pl.BlockSpec((None, te, d), lambda i, j: (layer, j, 0)),
        ],
        out_specs=pl.BlockSpec((tm, d), lambda i, j: (i, 0)),
        out_shape=jax.ShapeDtypeStruct((n, d), jnp.float32),
        scratch_shapes=[pltpu.VMEM((te, tm), jnp.float32), pltpu.VMEM((te, tm), jnp.bfloat16)],
        compiler_params=_params(("parallel", "arbitrary")),
        name="peer_dense",
    )(u, s1, s2, e1, e2, tau, eu, ev)


def _ple_kernel(h_ref, dl_ref, p_ref, g_ref, wg_ref, wp_ref, o_ref, h2_scr, u_scr, *, tn):
    j = pl.program_id(1)

    @pl.when(j == 0)
    def _():
        h2 = h_ref[...] + dl_ref[...]
        h2_scr[...] = h2
        u_scr[...] = _rms(h2, g_ref[...]).astype(jnp.bfloat16)

    gate = _sigmoid(jnp.dot(u_scr[...], wg_ref[...], preferred_element_type=jnp.float32))
    emb = jnp.dot(p_ref[...].astype(jnp.bfloat16), wp_ref[...],
                  preferred_element_type=jnp.float32)
    c0 = pl.multiple_of(j * tn, tn)
    o_ref[...] = h2_scr[:, pl.ds(c0, tn)] + emb * gate


def _ple(h, delta, p, g, wg, wp, layer, tm, tn):
    n, d = h.shape
    pd = p.shape[-1]
    return pl.pallas_call(
        functools.partial(_ple_kernel, tn=tn),
        grid=(n // tm, d // tn),
        in_specs=[
            pl.BlockSpec((tm, d), lambda i, j: (i, 0)),
            pl.BlockSpec((tm, d), lambda i, j: (i, 0)),
            pl.BlockSpec((None, tm, pd), lambda i, j: (layer, i, 0)),
            pl.BlockSpec((None, 1, d), lambda i, j: (layer, 0, 0)),
            pl.BlockSpec((None, d, tn), lambda i, j: (layer, 0, j)),
            pl.BlockSpec((None, pd, tn), lambda i, j: (layer, 0, j)),
        ],
        out_specs=pl.BlockSpec((tm, tn), lambda i, j: (i, j)),
        out_shape=jax.ShapeDtypeStruct((n, d), jnp.float32),
        scratch_shapes=[pltpu.VMEM((tm, d), jnp.float32), pltpu.VMEM((tm, d), jnp.bfloat16)],
        compiler_params=_params(("parallel", "arbitrary")),
        name="ple",
    )(h, delta, p, g, wg, wp)


def _final_norm_kernel(h_ref, g_ref, o_ref):
    o_ref[...] = _rms(h_ref[...], g_ref[...])


def _final_norm(h, g, tm):
    n, d = h.shape
    return pl.pallas_call(
        _final_norm_kernel,
        grid=(n // tm,),
        in_specs=[pl.BlockSpec((tm, d), lambda i: (i, 0)), pl.BlockSpec((1, d), lambda i: (0, 0))],
        out_specs=pl.BlockSpec((tm, d), lambda i: (i, 0)),
        out_shape=jax.ShapeDtypeStruct((n, d), jnp.float32),
        compiler_params=_params(("parallel",)),
        name="final_norm",
    )(h, g)


def kernel(x_prompt, x_sample, p_prompt, p_sample, state_conv_a, state_conv_b, norm1_g, w_in, conv_a_w, conv_a_b, ln_a_g, ln_a_b, conv_b_w, w_out, norm2_g, peer_wq, peer_k1, peer_k2, peer_u, peer_v, norm3_g, ple_w, ple_gate_w, final_g):
    batch, seq, d = x_prompt.shape
    dec = x_sample.shape[0]
    depth = w_in.shape[0]
    n_prompt = batch * seq
    n = n_prompt + dec
    assert d == D_MODEL and x_sample.shape[1] == 1
    assert seq % LANES == 0 and dec % LANES == 0 and seq >= A_HALO
    tm = _token_tile(n)
    tb = LANES
    bc = 32
    te = SUBLANES * N_KEYS
    ha = CONV_A_W - 1
    hb = CONV_B_W - 1
    bf16 = jnp.bfloat16

    h = jnp.concatenate([x_prompt.reshape(n_prompt, d), x_sample.reshape(dec, d)], axis=0)
    p_all = jnp.concatenate([p_prompt.reshape(depth, n_prompt, -1),
                             p_sample.reshape(depth, dec, -1)], axis=1)
    sa2 = state_conv_a.reshape(depth, dec, ha * WIDTH)
    sb2 = state_conv_b.reshape(depth, dec, hb * WIDTH)
    row = lambda v: v.reshape(depth, 1, v.shape[-1])
    w_in_b, w_out_b, wq_b = w_in.astype(bf16), w_out.astype(bf16), peer_wq.astype(bf16)
    eu_b, ev_b = peer_u.astype(bf16), peer_v.astype(bf16)
    wg_b, wp_b = ple_gate_w.astype(bf16), ple_w.astype(bf16)
    n1, n2, n3 = row(norm1_g), row(norm2_g), row(norm3_g)
    ba, lg, lb = row(conv_a_b), row(ln_a_g), row(ln_a_b)

    na_p, na_s, nb_p, nb_s = [], [], [], []
    for l in range(depth):
        proj = _norm_matmul(h, n1, w_in_b, l, tm, PROJ_IN // 4)
        mix, ta, tbb = _conv_prompt(proj, conv_a_w, ba, lg, lb, conv_b_w, l, batch, seq, tb)
        mix, sa_new, sb_new = _conv_sample(proj, sa2, sb2, conv_a_w, ba, lg, lb, conv_b_w,
                                           mix, l, n_prompt, bc)
        na_p.append(ta[:, A_HALO - ha:, :])
        nb_p.append(tbb[:, B_HALO - hb:, :])
        na_s.append(sa_new.reshape(dec, ha, WIDTH))
        nb_s.append(sb_new.reshape(dec, hb, WIDTH))
        h1 = _matmul_res(mix, w_out_b, h, l, tm, d // 2)
        u2, s1, s2 = _peer_scores(h1, n2, wq_b, peer_k1, peer_k2, l, tm)
        tau, e1, e2 = _peer_topk(s1, s2)
        delta = _peer(u2, s1, s2, e1, e2, tau, eu_b, ev_b, l, tm, te)
        h = _ple(h1, delta, p_all, n3, wg_b, wp_b, l, tm, d // 2)

    y = _final_norm(h, final_g.reshape(1, d), tm)
    y_prompt = y[:n_prompt].reshape(batch, seq, d)
    y_sample = y[n_prompt:].reshape(dec, 1, d)
    return (y_prompt, y_sample, jnp.stack(na_p), jnp.stack(na_s), jnp.stack(nb_p), jnp.stack(nb_s))
```

```python
import functools
import math

import jax
import jax.numpy as jnp
from jax import lax
from jax.experimental import pallas as pl
from jax.experimental.pallas import tpu as pltpu

D_MODEL = 2048
WIDTH = D_MODEL // 2
CONV_A_W = 31
CONV_B_W = 3
PROJ_IN = 5 * WIDTH
PEER_HEADS = 8
N_KEYS = 128
N_EXPERTS = N_KEYS * N_KEYS
HALF = 128
PEER_QDIM = 2 * HALF
TOPK = 16
RMS_EPS = 1e-6
LN_EPS = 1e-5

LANES = 128
SUBLANES = 8
A_HALO = 32
B_HALO = 8
VMEM_LIMIT = 56 * 1024 * 1024

_NT = (((1,), (1,)), ((), ()))
_TN = (((0,), (0,)), ((), ()))


def _params(sem):
    return pltpu.CompilerParams(dimension_semantics=sem, vmem_limit_bytes=VMEM_LIMIT)


MXU_TILE = 256
PEER_TOKEN_TILE = 3 * MXU_TILE


def _rms(x, g):
    ms = jnp.mean(x * x, axis=-1, keepdims=True)
    return x * lax.rsqrt(ms + RMS_EPS) * g


def _sigmoid(x):
    return 1.0 / (1.0 + jnp.exp(-x))


def _norm_matmul_kernel(x_ref, g_ref, w_ref, o_ref, u_scr):
    @pl.when(pl.program_id(1) == 0)
    def _():
        u_scr[...] = _rms(x_ref[...], g_ref[...]).astype(jnp.bfloat16)

    o_ref[...] = jnp.dot(u_scr[...], w_ref[...], preferred_element_type=jnp.float32)


def _norm_matmul(x, g, w, layer, tm, tn):
    n, d = x.shape
    e = w.shape[-1]
    return pl.pallas_call(
        _norm_matmul_kernel,
        grid=(n // tm, e // tn),
        in_specs=[
            pl.BlockSpec((tm, d), lambda i, j: (i, 0)),
            pl.BlockSpec((None, 1, d), lambda i, j: (layer, 0, 0)),
            pl.BlockSpec((None, d, tn), lambda i, j: (layer, 0, j)),
        ],
        out_specs=pl.BlockSpec((tm, tn), lambda i, j: (i, j)),
        out_shape=jax.ShapeDtypeStruct((n, e), jnp.float32),
        scratch_shapes=[pltpu.VMEM((tm, d), jnp.bfloat16)],
        compiler_params=_params(("parallel", "arbitrary")),
        name="norm_matmul",
    )(x, g, w)


def _ln_silu(y, g, b):
    mu = jnp.mean(y, axis=-1, keepdims=True)
    yc = y - mu
    var = jnp.mean(yc * yc, axis=-1, keepdims=True)
    ln = yc * lax.rsqrt(var + LN_EPS) * g + b
    return ln * _sigmoid(ln)


def _conv_prompt_kernel(p_ref, wa_ref, ba_ref, lg_ref, lb_ref, wb_ref,
                        mix_ref, na_ref, nb_ref, abuf, bbuf, ybuf, *, tb, cw):
    t = pl.program_id(1)

    @pl.when(t == 0)
    def _():
        abuf[0:A_HALO, :] = jnp.zeros((A_HALO, WIDTH), jnp.float32)
        bbuf[0:B_HALO, :] = jnp.zeros((B_HALO, WIDTH), jnp.float32)

    @pl.when(t > 0)
    def _():
        abuf[0:A_HALO, :] = abuf[tb:tb + A_HALO, :]
        bbuf[0:B_HALO, :] = bbuf[tb:tb + B_HALO, :]

    a_glu = p_ref[:, 0:WIDTH] * _sigmoid(p_ref[:, WIDTH:2 * WIDTH])
    abuf[A_HALO:A_HALO + tb, :] = a_glu
    off_a = A_HALO - (CONV_A_W - 1)
    for c0 in range(0, WIDTH, cw):
        acc = jnp.zeros((tb, cw), jnp.float32)
        for j in range(CONV_A_W):
            acc = acc + wa_ref[j:j + 1, c0:c0 + cw] * abuf[off_a + j:off_a + j + tb, c0:c0 + cw]
        ybuf[:, c0:c0 + cw] = acc + ba_ref[:, c0:c0 + cw]
    a = _ln_silu(ybuf[...], lg_ref[...], lb_ref[...])
    mix_ref[:, 0:WIDTH] = a.astype(mix_ref.dtype)

    cb = p_ref[:, 3 * WIDTH:4 * WIDTH] * p_ref[:, 4 * WIDTH:5 * WIDTH]
    bbuf[B_HALO:B_HALO + tb, :] = cb
    off_b = B_HALO - (CONV_B_W - 1)
    z = jnp.zeros((tb, WIDTH), jnp.float32)
    for j in range(CONV_B_W):
        z = z + wb_ref[j:j + 1, :] * bbuf[off_b + j:off_b + j + tb, :]
    mix_ref[:, WIDTH:2 * WIDTH] = (p_ref[:, 2 * WIDTH:3 * WIDTH] * z).astype(mix_ref.dtype)

    na_ref[...] = abuf[tb:tb + A_HALO, :]
    nb_ref[...] = bbuf[tb:tb + B_HALO, :]


def _conv_prompt(proj, wa, ba, lg, lb, wb, layer, batch, seq, tb):
    n = proj.shape[0]
    nt = seq // tb
    vec = lambda: pl.BlockSpec((None, 1, WIDTH), lambda b, t: (layer, 0, 0))
    return pl.pallas_call(
        functools.partial(_conv_prompt_kernel, tb=tb, cw=2 * LANES),
        grid=(batch, nt),
        in_specs=[
            pl.BlockSpec((tb, PROJ_IN), lambda b, t: (b * nt + t, 0)),
            pl.BlockSpec((None, CONV_A_W, WIDTH), lambda b, t: (layer, 0, 0)),
            vec(), vec(), vec(),
            pl.BlockSpec((None, CONV_B_W, WIDTH), lambda b, t: (layer, 0, 0)),
        ],
        out_specs=[
            pl.BlockSpec((tb, D_MODEL), lambda b, t: (b * nt + t, 0)),
            pl.BlockSpec((None, A_HALO, WIDTH), lambda b, t: (b, 0, 0)),
            pl.BlockSpec((None, B_HALO, WIDTH), lambda b, t: (b, 0, 0)),
        ],
        out_shape=[
            jax.ShapeDtypeStruct((n, D_MODEL), jnp.bfloat16),
            jax.ShapeDtypeStruct((batch, A_HALO, WIDTH), jnp.float32),
            jax.ShapeDtypeStruct((batch, B_HALO, WIDTH), jnp.float32),
        ],
        scratch_shapes=[
            pltpu.VMEM((A_HALO + tb, WIDTH), jnp.float32),
            pltpu.VMEM((B_HALO + tb, WIDTH), jnp.float32),
            pltpu.VMEM((tb, WIDTH), jnp.float32),
        ],
        compiler_params=_params(("parallel", "arbitrary")),
        name="conv_prompt",
    )(proj, wa, ba, lg, lb, wb)


def _conv_sample_kernel(p_ref, sa_ref, sb_ref, wa_ref, ba_ref, lg_ref, lb_ref, wb_ref, mix_in,
                        mix_ref, na_ref, nb_ref, *, n_real):
    del mix_in

    @pl.when(pl.program_id(0) >= n_real)
    def _():
        mix_ref[...] = jnp.zeros_like(mix_ref)

    @pl.when(pl.program_id(0) < n_real)
    def _():
        _conv_sample_rows(p_ref, sa_ref, sb_ref, wa_ref, ba_ref, lg_ref, lb_ref, wb_ref,
                          mix_ref, na_ref, nb_ref)


def _conv_sample_rows(p_ref, sa_ref, sb_ref, wa_ref, ba_ref, lg_ref, lb_ref, wb_ref,
                      mix_ref, na_ref, nb_ref):
    ha = CONV_A_W - 1
    hb = CONV_B_W - 1
    a_glu = p_ref[:, 0:WIDTH] * _sigmoid(p_ref[:, WIDTH:2 * WIDTH])
    acc = wa_ref[ha:ha + 1, :] * a_glu
    for j in range(ha):
        acc = acc + wa_ref[j:j + 1, :] * sa_ref[:, j * WIDTH:(j + 1) * WIDTH]
    a = _ln_silu(acc + ba_ref[...], lg_ref[...], lb_ref[...])
    mix_ref[:, 0:WIDTH] = a.astype(mix_ref.dtype)
    na_ref[:, 0:(ha - 1) * WIDTH] = sa_ref[:, WIDTH:ha * WIDTH]
    na_ref[:, (ha - 1) * WIDTH:ha * WIDTH] = a_glu

    cb = p_ref[:, 3 * WIDTH:4 * WIDTH] * p_ref[:, 4 * WIDTH:5 * WIDTH]
    z = wb_ref[hb:hb + 1, :] * cb
    for j in range(hb):
        z = z + wb_ref[j:j + 1, :] * sb_ref[:, j * WIDTH:(j + 1) * WIDTH]
    mix_ref[:, WIDTH:2 * WIDTH] = (p_ref[:, 2 * WIDTH:3 * WIDTH] * z).astype(mix_ref.dtype)
    nb_ref[:, 0:(hb - 1) * WIDTH] = sb_ref[:, WIDTH:hb * WIDTH]
    nb_ref[:, (hb - 1) * WIDTH:hb * WIDTH] = cb


def _conv_sample(proj, sa2, sb2, wa, ba, lg, lb, wb, mix, layer, n_prompt, bc):
    dec = sa2.shape[1]
    ha = CONV_A_W - 1
    hb = CONV_B_W - 1
    r0 = n_prompt // bc
    n_real = dec // bc
    n_steps = (mix.shape[0] - n_prompt) // bc
    real = lambda i: jnp.minimum(i, n_real - 1)
    vec = lambda: pl.BlockSpec((None, 1, WIDTH), lambda i: (layer, 0, 0))
    return pl.pallas_call(
        functools.partial(_conv_sample_kernel, n_real=n_real),
        grid=(n_steps,),
        in_specs=[
            pl.BlockSpec((bc, PROJ_IN), lambda i: (r0 + i, 0)),
            pl.BlockSpec((None, bc, ha * WIDTH), lambda i: (layer, real(i), 0)),
            pl.BlockSpec((None, bc, hb * WIDTH), lambda i: (layer, real(i), 0)),
            pl.BlockSpec((None, CONV_A_W, WIDTH), lambda i: (layer, 0, 0)),
            vec(), vec(), vec(),
            pl.BlockSpec((None, CONV_B_W, WIDTH), lambda i: (layer, 0, 0)),
            pl.BlockSpec(memory_space=pl.ANY),
        ],
        out_specs=[
            pl.BlockSpec((bc, D_MODEL), lambda i: (r0 + i, 0)),
            pl.BlockSpec((bc, ha * WIDTH), lambda i: (real(i), 0)),
            pl.BlockSpec((bc, hb * WIDTH), lambda i: (real(i), 0)),
        ],
        out_shape=[
            jax.ShapeDtypeStruct(mix.shape, mix.dtype),
            jax.ShapeDtypeStruct((dec, ha * WIDTH), jnp.float32),
            jax.ShapeDtypeStruct((dec, hb * WIDTH), jnp.float32),
        ],
        input_output_aliases={8: 0},
        compiler_params=_params(("arbitrary",)),
        name="conv_sample",
    )(proj, sa2, sb2, wa, ba, lg, lb, wb, mix)


def _matmul_res_kernel(m_ref, w_ref, h_ref, o_ref):
    o_ref[...] = h_ref[...] + jnp.dot(m_ref[...], w_ref[...],
                                      preferred_element_type=jnp.float32)


def _matmul_res(mix, w, h, layer, tm, tn):
    n, k = mix.shape
    d = w.shape[-1]
    return pl.pallas_call(
        _matmul_res_kernel,
        grid=(n // tm, d // tn),
        in_specs=[
            pl.BlockSpec((tm, k), lambda i, j: (i, 0)),
            pl.BlockSpec((None, k, tn), lambda i, j: (layer, 0, j)),
            pl.BlockSpec((tm, tn), lambda i, j: (i, j)),
        ],
        out_specs=pl.BlockSpec((tm, tn), lambda i, j: (i, j)),
        out_shape=jax.ShapeDtypeStruct((n, d), jnp.float32),
        compiler_params=_params(("parallel", "arbitrary")),
        name="matmul_res",
    )(mix, w, h)


def _peer_scores_kernel(h_ref, g_ref, wq_ref, k1_ref, k2_ref, ut_ref, s1_ref, s2_ref, s1k_ref,
                        q_scr):
    u32 = _rms(h_ref[...], g_ref[...])
    ut_ref[...] = u32.T.astype(jnp.bfloat16)
    q_scr[...] = jnp.dot(u32.astype(jnp.bfloat16), wq_ref[...],
                         preferred_element_type=jnp.float32)
    k1 = k1_ref[...].astype(jnp.bfloat16)
    k2 = k2_ref[...].astype(jnp.bfloat16)
    for h in range(PEER_HEADS):
        c0 = h * PEER_QDIM
        q1 = q_scr[:, c0:c0 + HALF].astype(jnp.bfloat16)
        q2 = q_scr[:, c0 + HALF:c0 + PEER_QDIM].astype(jnp.bfloat16)
        s1 = lax.dot_general(k1, q1, _NT, preferred_element_type=jnp.float32)
        s1_ref[h] = s1
        s1k_ref[:, h, :] = s1
        s2_ref[h] = lax.dot_general(k2, q2, _NT, preferred_element_type=jnp.float32)


def _peer_scores(h, g, wq, k1, k2, layer, tm):
    n, d = h.shape
    qd = wq.shape[-1]
    sshape = jax.ShapeDtypeStruct((PEER_HEADS, N_KEYS, n), jnp.float32)
    sspec = lambda: pl.BlockSpec((PEER_HEADS, N_KEYS, tm), lambda i: (0, 0, i))
    kshape = jax.ShapeDtypeStruct((N_KEYS, PEER_HEADS, n), jnp.float32)
    kspec = pl.BlockSpec((N_KEYS, PEER_HEADS, tm), lambda i: (0, 0, i))
    return pl.pallas_call(
        _peer_scores_kernel,
        grid=(n // tm,),
        in_specs=[
            pl.BlockSpec((tm, d), lambda i: (i, 0)),
            pl.BlockSpec((None, 1, d), lambda i: (layer, 0, 0)),
            pl.BlockSpec((None, d, qd), lambda i: (layer, 0, 0)),
            pl.BlockSpec((None, N_KEYS, HALF), lambda i: (layer, 0, 0)),
            pl.BlockSpec((None, N_KEYS, HALF), lambda i: (layer, 0, 0)),
        ],
        out_specs=[pl.BlockSpec((d, tm), lambda i: (0, i)), sspec(), sspec(), kspec],
        out_shape=[jax.ShapeDtypeStruct((d, n), jnp.bfloat16), sshape, sshape, kshape],
        scratch_shapes=[pltpu.VMEM((tm, qd), jnp.float32)],
        compiler_params=_params(("parallel",)),
        name="peer_scores",
    )(h, g, wq, k1, k2)


def _top_values(x, k):
    out = []
    for _ in range(k):
        m = jnp.max(x, axis=0, keepdims=True)
        out.append(m)
        x = jnp.where(x == m, -jnp.inf, x)
    return out


def _peer_topk_kernel(s1_ref, s2_ref, tau_ref, e1k_ref, e2_ref):
    for h in range(PEER_HEADS):
        s1 = s1_ref[h]
        s2 = s2_ref[h]
        v1 = _top_values(s1, TOPK)
        v2 = jnp.concatenate(_top_values(s2, TOPK), axis=0)
        cand = [v1[a] + v2[0:TOPK // (a + 1), :] for a in range(TOPK)]
        n_cand = sum(c.shape[0] for c in cand)
        pad = -n_cand % SUBLANES
        if pad:
            cand.append(jnp.full((pad, LANES), -jnp.inf, jnp.float32))
        sc = _top_values(jnp.concatenate(cand, axis=0), TOPK)
        top = sc[0]
        z = jnp.zeros_like(top)
        for c in sc:
            z = z + jnp.exp(c - top)
        tau_ref[h:h + 1, :] = sc[TOPK - 1]
        e1k_ref[:, h, :] = jnp.exp(s1 - v1[0]) * (1.0 / z)
        e2_ref[h] = jnp.exp(s2 - v2[0:1, :])


def _peer_topk(s1, s2):
    n = s1.shape[-1]
    sspec = lambda: pl.BlockSpec((PEER_HEADS, N_KEYS, LANES), lambda i: (0, 0, i))
    kspec = pl.BlockSpec((N_KEYS, PEER_HEADS, LANES), lambda i: (0, 0, i))
    return pl.pallas_call(
        _peer_topk_kernel,
        grid=(n // LANES,),
        in_specs=[sspec(), sspec()],
        out_specs=[pl.BlockSpec((PEER_HEADS, LANES), lambda i: (0, i)), kspec, sspec()],
        out_shape=[
            jax.ShapeDtypeStruct((PEER_HEADS, n), jnp.float32),
            jax.ShapeDtypeStruct((N_KEYS, PEER_HEADS, n), jnp.float32),
            jax.ShapeDtypeStruct(s1.shape, jnp.float32),
        ],
        compiler_params=_params(("parallel",)),
        name="peer_topk",
    )(s1, s2)


def _peer_kernel(ut_ref, s1p_ref, s1c_ref, e1p_ref, e1c_ref, s2_ref, e2_ref, tau_ref, eu_ref,
                 evt_ref, o_ref, a0, a1, w0, w1, acc, *, tm, nb):
    j = pl.program_id(1)
    half = a0.shape[0]
    n_i1 = half // N_KEYS

    @pl.when(j == 0)
    def _():
        acc[...] = jnp.zeros_like(acc)
        w0[...] = jnp.zeros_like(w0)
        a1[...] = jnp.zeros_like(a1)

    def unit(e0, a_dst, a_src, w_dst, w_src, s1_ref, e1_ref, row0, valid):
        eh = pl.ds(e0, half)
        a_dst[...] = jnp.dot(eu_ref[eh, :], ut_ref[...],
                             preferred_element_type=jnp.float32)
        for lg in range(tm // LANES):
            ls = slice(lg * LANES, (lg + 1) * LANES)
            taus = [jnp.where(valid, tau_ref[h:h + 1, ls], jnp.inf) for h in range(PEER_HEADS)]
            for il in range(n_i1):
                r = row0 + il
                er = slice(il * N_KEYS, (il + 1) * N_KEYS)
                g = jnp.zeros((N_KEYS, LANES), jnp.float32)
                for h in range(PEER_HEADS):
                    c = s2_ref[h, :, ls] + s1_ref[r, h:h + 1, ls]
                    e = e2_ref[h, :, ls] * e1_ref[r, h:h + 1, ls]
                    g = g + jnp.where(c >= taus[h], e, 0.0)
                a = a_src[er, ls]
                act = 0.5 * a * (1.0 + lax.erf(a * math.sqrt(0.5)))
                w_dst[er, ls] = (g * act).astype(w_dst.dtype)
        acc[...] += jnp.dot(evt_ref[:, eh], w_src[...],
                            preferred_element_type=jnp.float32)

    unit(0, a0, a1, w1, w0, s1p_ref, e1p_ref, n_i1, j > 0)

    @pl.when(j >= 0)
    def _():
        unit(half, a1, a0, w0, w1, s1c_ref, e1c_ref, 0, j < nb)

    @pl.when(j == nb)
    def _():
        o_ref[...] = acc[...].T


def _peer(ut, s1k, s2, e1k, e2, tau, eu, evt, layer, tm):
    d, n = ut.shape
    te = SUBLANES * N_KEYS
    nb = N_EXPERTS // te
    once = pl.Buffered(1)
    prev = lambda j: jnp.maximum(j - 1, 0)
    cur = lambda j: jnp.minimum(j, nb - 1)
    rows_p = lambda: pl.BlockSpec((SUBLANES, PEER_HEADS, tm), lambda i, j: (prev(j), 0, i))
    rows_c = lambda: pl.BlockSpec((SUBLANES, PEER_HEADS, tm), lambda i, j: (cur(j), 0, i))
    full = lambda: pl.BlockSpec((PEER_HEADS, N_KEYS, tm), lambda i, j: (0, 0, i),
                                pipeline_mode=once)
    return pl.pallas_call(
        functools.partial(_peer_kernel, tm=tm, nb=nb),
        grid=(n // tm, nb + 1),
        in_specs=[
            pl.BlockSpec((d, tm), lambda i, j: (0, i), pipeline_mode=once),
            rows_p(), rows_c(), rows_p(), rows_c(),
            full(), full(),
            pl.BlockSpec((PEER_HEADS, tm), lambda i, j: (0, i)),
            pl.BlockSpec((None, te, d), lambda i, j: (layer, cur(j), 0)),
            pl.BlockSpec((None, d, te), lambda i, j: (layer, 0, prev(j))),
        ],
        out_specs=pl.BlockSpec((tm, d), lambda i, j: (i, 0), pipeline_mode=once),
        out_shape=jax.ShapeDtypeStruct((n, d), jnp.float32),
        scratch_shapes=[pltpu.VMEM((te // 2, tm), jnp.float32), pltpu.VMEM((te // 2, tm), jnp.float32),
                        pltpu.VMEM((te // 2, tm), jnp.bfloat16), pltpu.VMEM((te // 2, tm), jnp.bfloat16),
                        pltpu.VMEM((d, tm), jnp.float32)],
        compiler_params=_params(("parallel", "arbitrary")),
        name="peer_dense",
    )(ut, s1k, s1k, e1k, e1k, s2, e2, tau, eu, evt)


def _ple_kernel(h_ref, dl_ref, p_ref, g_ref, wg_ref, wp_ref, o_ref, h2_scr, u_scr, *, tn):
    j = pl.program_id(1)

    @pl.when(j == 0)
    def _():
        h2 = h_ref[...] + dl_ref[...]
        h2_scr[...] = h2
        u_scr[...] = _rms(h2, g_ref[...]).astype(jnp.bfloat16)

    gate = _sigmoid(jnp.dot(u_scr[...], wg_ref[...], preferred_element_type=jnp.float32))
    emb = jnp.dot(p_ref[...].astype(jnp.bfloat16), wp_ref[...],
                  preferred_element_type=jnp.float32)
    c0 = pl.multiple_of(j * tn, tn)
    o_ref[...] = h2_scr[:, pl.ds(c0, tn)] + emb * gate


def _ple(h, delta, p, g, wg, wp, layer, tm, tn):
    n, d = h.shape
    pd = p.shape[-1]
    return pl.pallas_call(
        functools.partial(_ple_kernel, tn=tn),
        grid=(n // tm, d // tn),
        in_specs=[
            pl.BlockSpec((tm, d), lambda i, j: (i, 0)),
            pl.BlockSpec((tm, d), lambda i, j: (i, 0)),
            pl.BlockSpec((None, tm, pd), lambda i, j: (layer, i, 0)),
            pl.BlockSpec((None, 1, d), lambda i, j: (layer, 0, 0)),
            pl.BlockSpec((None, d, tn), lambda i, j: (layer, 0, j)),
            pl.BlockSpec((None, pd, tn), lambda i, j: (layer, 0, j)),
        ],
        out_specs=pl.BlockSpec((tm, tn), lambda i, j: (i, j)),
        out_shape=jax.ShapeDtypeStruct((n, d), jnp.float32),
        scratch_shapes=[pltpu.VMEM((tm, d), jnp.float32), pltpu.VMEM((tm, d), jnp.bfloat16)],
        compiler_params=_params(("parallel", "arbitrary")),
        name="ple",
    )(h, delta, p, g, wg, wp)


def _final_norm_kernel(h_ref, g_ref, o_ref):
    o_ref[...] = _rms(h_ref[...], g_ref[...])


def _final_norm(h, g, tm):
    n, d = h.shape
    return pl.pallas_call(
        _final_norm_kernel,
        grid=(n // tm,),
        in_specs=[pl.BlockSpec((tm, d), lambda i: (i, 0)), pl.BlockSpec((1, d), lambda i: (0, 0))],
        out_specs=pl.BlockSpec((tm, d), lambda i: (i, 0)),
        out_shape=jax.ShapeDtypeStruct((n, d), jnp.float32),
        compiler_params=_params(("parallel",)),
        name="final_norm",
    )(h, g)


def kernel(x_prompt, x_sample, p_prompt, p_sample, state_conv_a, state_conv_b, norm1_g, w_in, conv_a_w, conv_a_b, ln_a_g, ln_a_b, conv_b_w, w_out, norm2_g, peer_wq, peer_k1, peer_k2, peer_u, peer_v, norm3_g, ple_w, ple_gate_w, final_g):
    batch, seq, d = x_prompt.shape
    dec = x_sample.shape[0]
    depth = w_in.shape[0]
    n_prompt = batch * seq
    n = n_prompt + dec
    assert d == D_MODEL and x_sample.shape[1] == 1
    assert seq % LANES == 0 and dec % LANES == 0 and seq >= A_HALO
    tp = PEER_TOKEN_TILE
    tm = tp // 2
    n_pad = -(-n // tp) * tp
    tb = LANES
    bc = 32
    ha = CONV_A_W - 1
    hb = CONV_B_W - 1
    bf16 = jnp.bfloat16

    pd = p_prompt.shape[-1]
    h = jnp.concatenate([x_prompt.reshape(n_prompt, d), x_sample.reshape(dec, d),
                         jnp.zeros((n_pad - n, d), x_prompt.dtype)], axis=0)
    p_all = jnp.concatenate([p_prompt.reshape(depth, n_prompt, pd),
                             p_sample.reshape(depth, dec, pd),
                             jnp.zeros((depth, n_pad - n, pd), p_prompt.dtype)], axis=1)
    sa2 = state_conv_a.reshape(depth, dec, ha * WIDTH)
    sb2 = state_conv_b.reshape(depth, dec, hb * WIDTH)
    row = lambda v: v.reshape(depth, 1, v.shape[-1])
    w_in_b, w_out_b, wq_b = w_in.astype(bf16), w_out.astype(bf16), peer_wq.astype(bf16)
    eu_b, evt_b = peer_u.astype(bf16), jnp.swapaxes(peer_v, 1, 2).astype(bf16)
    wg_b, wp_b = ple_gate_w.astype(bf16), ple_w.astype(bf16)
    n1, n2, n3 = row(norm1_g), row(norm2_g), row(norm3_g)
    ba, lg, lb = row(conv_a_b), row(ln_a_g), row(ln_a_b)

    na_p, na_s, nb_p, nb_s = [], [], [], []
    for l in range(depth):
        proj = _norm_matmul(h, n1, w_in_b, l, tm, PROJ_IN // 4)
        mix, ta, tbb = _conv_prompt(proj, conv_a_w, ba, lg, lb, conv_b_w, l, batch, seq, tb)
        mix, sa_new, sb_new = _conv_sample(proj, sa2, sb2, conv_a_w, ba, lg, lb, conv_b_w,
                                           mix, l, n_prompt, bc)
        na_p.append(ta[:, A_HALO - ha:, :])
        nb_p.append(tbb[:, B_HALO - hb:, :])
        na_s.append(sa_new.reshape(dec, ha, WIDTH))
        nb_s.append(sb_new.reshape(dec, hb, WIDTH))
        h1 = _matmul_res(mix, w_out_b, h, l, tm, d // 2)
        u2, s1, s2, s1k = _peer_scores(h1, n2, wq_b, peer_k1, peer_k2, l, tm)
        tau, e1k, e2 = _peer_topk(s1, s2)
        delta = _peer(u2, s1k, s2, e1k, e2, tau, eu_b, evt_b, l, tp)
        h = _ple(h1, delta, p_all, n3, wg_b, wp_b, l, tm, d // 2)

    y = _final_norm(h, final_g.reshape(1, d), tm)
    y_prompt = y[:n_prompt].reshape(batch, seq, d)
    y_sample = y[n_prompt:n].reshape(dec, 1, d)
    return (y_prompt, y_sample, jnp.stack(na_p), jnp.stack(na_s), jnp.stack(nb_p), jnp.stack(nb_s))
```

```python
import functools
import math

import jax
import jax.numpy as jnp
from jax import lax
from jax.experimental import pallas as pl
from jax.experimental.pallas import tpu as pltpu

D_MODEL = 2048
WIDTH = D_MODEL // 2
CONV_A_W = 31
CONV_B_W = 3
PROJ_IN = 5 * WIDTH
PEER_HEADS = 8
N_KEYS = 128
N_EXPERTS = N_KEYS * N_KEYS
HALF = 128
PEER_QDIM = 2 * HALF
TOPK = 16
RMS_EPS = 1e-6
LN_EPS = 1e-5

LANES = 128
SUBLANES = 8
A_HALO = 32
B_HALO = 8
VMEM_LIMIT = 56 * 1024 * 1024

_NT = (((1,), (1,)), ((), ()))
_TN = (((0,), (0,)), ((), ()))


def _params(sem):
    return pltpu.CompilerParams(dimension_semantics=sem, vmem_limit_bytes=VMEM_LIMIT)


MXU_TILE = 256
PEER_TOKEN_TILE = 3 * MXU_TILE


def _rms(x, g):
    ms = jnp.mean(x * x, axis=-1, keepdims=True)
    return x * lax.rsqrt(ms + RMS_EPS) * g


def _sigmoid(x):
    return 1.0 / (1.0 + jnp.exp(-x))


def _norm_matmul_kernel(x_ref, g_ref, w_ref, o_ref, u_scr):
    @pl.when(pl.program_id(1) == 0)
    def _():
        u_scr[...] = _rms(x_ref[...], g_ref[...]).astype(jnp.bfloat16)

    o_ref[...] = jnp.dot(u_scr[...], w_ref[...], preferred_element_type=jnp.float32)


def _norm_matmul(x, g, w, layer, tm, tn):
    n, d = x.shape
    e = w.shape[-1]
    return pl.pallas_call(
        _norm_matmul_kernel,
        grid=(n // tm, e // tn),
        in_specs=[
            pl.BlockSpec((tm, d), lambda i, j: (i, 0)),
            pl.BlockSpec((None, 1, d), lambda i, j: (layer, 0, 0)),
            pl.BlockSpec((None, d, tn), lambda i, j: (layer, 0, j)),
        ],
        out_specs=pl.BlockSpec((tm, tn), lambda i, j: (i, j)),
        out_shape=jax.ShapeDtypeStruct((n, e), jnp.float32),
        scratch_shapes=[pltpu.VMEM((tm, d), jnp.bfloat16)],
        compiler_params=_params(("parallel", "arbitrary")),
        name="norm_matmul",
    )(x, g, w)


def _ln_silu(y, g, b):
    mu = jnp.mean(y, axis=-1, keepdims=True)
    yc = y - mu
    var = jnp.mean(yc * yc, axis=-1, keepdims=True)
    ln = yc * lax.rsqrt(var + LN_EPS) * g + b
    return ln * _sigmoid(ln)


def _conv_prompt_kernel(p_ref, wa_ref, ba_ref, lg_ref, lb_ref, wb_ref,
                        mix_ref, na_ref, nb_ref, abuf, bbuf, ybuf, *, tb, cw, nt, n_real):
    s = pl.program_id(0)

    @pl.when(s >= n_real)
    def _():
        mix_ref[...] = jnp.zeros_like(mix_ref)

    @pl.when(s < n_real)
    def _():
        _conv_prompt_block(p_ref, wa_ref, ba_ref, lg_ref, lb_ref, wb_ref, mix_ref, na_ref, nb_ref,
                           abuf, bbuf, ybuf, s % nt, tb, cw)


def _conv_prompt_block(p_ref, wa_ref, ba_ref, lg_ref, lb_ref, wb_ref, mix_ref, na_ref, nb_ref,
                       abuf, bbuf, ybuf, t, tb, cw):
    @pl.when(t == 0)
    def _():
        abuf[0:A_HALO, :] = jnp.zeros((A_HALO, WIDTH), jnp.float32)
        bbuf[0:B_HALO, :] = jnp.zeros((B_HALO, WIDTH), jnp.float32)

    @pl.when(t > 0)
    def _():
        abuf[0:A_HALO, :] = abuf[tb:tb + A_HALO, :]
        bbuf[0:B_HALO, :] = bbuf[tb:tb + B_HALO, :]

    a_glu = p_ref[:, 0:WIDTH] * _sigmoid(p_ref[:, WIDTH:2 * WIDTH])
    abuf[A_HALO:A_HALO + tb, :] = a_glu
    off_a = A_HALO - (CONV_A_W - 1)
    for c0 in range(0, WIDTH, cw):
        acc = jnp.zeros((tb, cw), jnp.float32)
        for j in range(CONV_A_W):
            acc = acc + wa_ref[j:j + 1, c0:c0 + cw] * abuf[off_a + j:off_a + j + tb, c0:c0 + cw]
        ybuf[:, c0:c0 + cw] = acc + ba_ref[:, c0:c0 + cw]
    a = _ln_silu(ybuf[...], lg_ref[...], lb_ref[...])
    mix_ref[:, 0:WIDTH] = a.astype(mix_ref.dtype)

    cb = p_ref[:, 3 * WIDTH:4 * WIDTH] * p_ref[:, 4 * WIDTH:5 * WIDTH]
    bbuf[B_HALO:B_HALO + tb, :] = cb
    off_b = B_HALO - (CONV_B_W - 1)
    z = jnp.zeros((tb, WIDTH), jnp.float32)
    for j in range(CONV_B_W):
        z = z + wb_ref[j:j + 1, :] * bbuf[off_b + j:off_b + j + tb, :]
    mix_ref[:, WIDTH:2 * WIDTH] = (p_ref[:, 2 * WIDTH:3 * WIDTH] * z).astype(mix_ref.dtype)

    na_ref[...] = abuf[tb:tb + A_HALO, :]
    nb_ref[...] = bbuf[tb:tb + B_HALO, :]


def _conv_prompt(proj, wa, ba, lg, lb, wb, layer, batch, seq, tb):
    n = proj.shape[0]
    nt = seq // tb
    n_real = batch * nt
    seq_of = lambda s: jnp.minimum(s // nt, batch - 1)
    vec = lambda: pl.BlockSpec((None, 1, WIDTH), lambda s: (layer, 0, 0))
    return pl.pallas_call(
        functools.partial(_conv_prompt_kernel, tb=tb, cw=2 * LANES, nt=nt, n_real=n_real),
        grid=(n // tb,),
        in_specs=[
            pl.BlockSpec((tb, PROJ_IN), lambda s: (s, 0)),
            pl.BlockSpec((None, CONV_A_W, WIDTH), lambda s: (layer, 0, 0)),
            vec(), vec(), vec(),
            pl.BlockSpec((None, CONV_B_W, WIDTH), lambda s: (layer, 0, 0)),
        ],
        out_specs=[
            pl.BlockSpec((tb, D_MODEL), lambda s: (s, 0)),
            pl.BlockSpec((None, A_HALO, WIDTH), lambda s: (seq_of(s), 0, 0)),
            pl.BlockSpec((None, B_HALO, WIDTH), lambda s: (seq_of(s), 0, 0)),
        ],
        out_shape=[
            jax.ShapeDtypeStruct((n, D_MODEL), jnp.bfloat16),
            jax.ShapeDtypeStruct((batch, A_HALO, WIDTH), jnp.float32),
            jax.ShapeDtypeStruct((batch, B_HALO, WIDTH), jnp.float32),
        ],
        scratch_shapes=[
            pltpu.VMEM((A_HALO + tb, WIDTH), jnp.float32),
            pltpu.VMEM((B_HALO + tb, WIDTH), jnp.float32),
            pltpu.VMEM((tb, WIDTH), jnp.float32),
        ],
        compiler_params=_params(("arbitrary",)),
        name="conv_prompt",
    )(proj, wa, ba, lg, lb, wb)


def _conv_sample_kernel(p_ref, sa_ref, sb_ref, wa_ref, ba_ref, lg_ref, lb_ref, wb_ref, mix_in,
                        mix_ref, na_ref, nb_ref):
    del mix_in
    ha = CONV_A_W - 1
    hb = CONV_B_W - 1
    a_glu = p_ref[:, 0:WIDTH] * _sigmoid(p_ref[:, WIDTH:2 * WIDTH])
    acc = wa_ref[ha:ha + 1, :] * a_glu
    for j in range(ha):
        acc = acc + wa_ref[j:j + 1, :] * sa_ref[:, j * WIDTH:(j + 1) * WIDTH]
    a = _ln_silu(acc + ba_ref[...], lg_ref[...], lb_ref[...])
    mix_ref[:, 0:WIDTH] = a.astype(mix_ref.dtype)
    na_ref[:, 0:(ha - 1) * WIDTH] = sa_ref[:, WIDTH:ha * WIDTH]
    na_ref[:, (ha - 1) * WIDTH:ha * WIDTH] = a_glu

    cb = p_ref[:, 3 * WIDTH:4 * WIDTH] * p_ref[:, 4 * WIDTH:5 * WIDTH]
    z = wb_ref[hb:hb + 1, :] * cb
    for j in range(hb):
        z = z + wb_ref[j:j + 1, :] * sb_ref[:, j * WIDTH:(j + 1) * WIDTH]
    mix_ref[:, WIDTH:2 * WIDTH] = (p_ref[:, 2 * WIDTH:3 * WIDTH] * z).astype(mix_ref.dtype)
    nb_ref[:, 0:(hb - 1) * WIDTH] = sb_ref[:, WIDTH:hb * WIDTH]
    nb_ref[:, (hb - 1) * WIDTH:hb * WIDTH] = cb


def _conv_sample(proj, sa2, sb2, wa, ba, lg, lb, wb, mix, layer, n_prompt, bc):
    dec = sa2.shape[1]
    ha = CONV_A_W - 1
    hb = CONV_B_W - 1
    r0 = n_prompt // bc
    vec = lambda: pl.BlockSpec((None, 1, WIDTH), lambda i: (layer, 0, 0))
    return pl.pallas_call(
        _conv_sample_kernel,
        grid=(dec // bc,),
        in_specs=[
            pl.BlockSpec((bc, PROJ_IN), lambda i: (r0 + i, 0)),
            pl.BlockSpec((None, bc, ha * WIDTH), lambda i: (layer, i, 0)),
            pl.BlockSpec((None, bc, hb * WIDTH), lambda i: (layer, i, 0)),
            pl.BlockSpec((None, CONV_A_W, WIDTH), lambda i: (layer, 0, 0)),
            vec(), vec(), vec(),
            pl.BlockSpec((None, CONV_B_W, WIDTH), lambda i: (layer, 0, 0)),
            pl.BlockSpec(memory_space=pl.ANY),
        ],
        out_specs=[
            pl.BlockSpec((bc, D_MODEL), lambda i: (r0 + i, 0)),
            pl.BlockSpec((bc, ha * WIDTH), lambda i: (i, 0)),
            pl.BlockSpec((bc, hb * WIDTH), lambda i: (i, 0)),
        ],
        out_shape=[
            jax.ShapeDtypeStruct(mix.shape, mix.dtype),
            jax.ShapeDtypeStruct((dec, ha * WIDTH), jnp.float32),
            jax.ShapeDtypeStruct((dec, hb * WIDTH), jnp.float32),
        ],
        input_output_aliases={8: 0},
        compiler_params=_params(("parallel",)),
        name="conv_sample",
    )(proj, sa2, sb2, wa, ba, lg, lb, wb, mix)


def _matmul_res_kernel(m_ref, w_ref, h_ref, o_ref):
    o_ref[...] = h_ref[...] + jnp.dot(m_ref[...], w_ref[...],
                                      preferred_element_type=jnp.float32)


def _matmul_res(mix, w, h, layer, tm, tn):
    n, k = mix.shape
    d = w.shape[-1]
    return pl.pallas_call(
        _matmul_res_kernel,
        grid=(n // tm, d // tn),
        in_specs=[
            pl.BlockSpec((tm, k), lambda i, j: (i, 0)),
            pl.BlockSpec((None, k, tn), lambda i, j: (layer, 0, j)),
            pl.BlockSpec((tm, tn), lambda i, j: (i, j)),
        ],
        out_specs=pl.BlockSpec((tm, tn), lambda i, j: (i, j)),
        out_shape=jax.ShapeDtypeStruct((n, d), jnp.float32),
        compiler_params=_params(("parallel", "arbitrary")),
        name="matmul_res",
    )(mix, w, h)


def _peer_scores_kernel(h_ref, g_ref, wq_ref, k1_ref, k2_ref, ut_ref, s1_ref, s2_ref, q_scr):
    u32 = _rms(h_ref[...], g_ref[...])
    ut_ref[...] = u32.T.astype(jnp.bfloat16)
    q_scr[...] = jnp.dot(u32.astype(jnp.bfloat16), wq_ref[...],
                         preferred_element_type=jnp.float32)
    k1 = k1_ref[...].astype(jnp.bfloat16)
    k2 = k2_ref[...].astype(jnp.bfloat16)
    for h in range(PEER_HEADS):
        c0 = h * PEER_QDIM
        q1 = q_scr[:, c0:c0 + HALF].astype(jnp.bfloat16)
        q2 = q_scr[:, c0 + HALF:c0 + PEER_QDIM].astype(jnp.bfloat16)
        s1_ref[h] = lax.dot_general(k1, q1, _NT, preferred_element_type=jnp.float32)
        s2_ref[h] = lax.dot_general(k2, q2, _NT, preferred_element_type=jnp.float32)


def _peer_scores(h, g, wq, k1, k2, layer, tm):
    n, d = h.shape
    qd = wq.shape[-1]
    sshape = jax.ShapeDtypeStruct((PEER_HEADS, N_KEYS, n), jnp.float32)
    sspec = lambda: pl.BlockSpec((PEER_HEADS, N_KEYS, tm), lambda i: (0, 0, i))
    return pl.pallas_call(
        _peer_scores_kernel,
        grid=(n // tm,),
        in_specs=[
            pl.BlockSpec((tm, d), lambda i: (i, 0)),
            pl.BlockSpec((None, 1, d), lambda i: (layer, 0, 0)),
            pl.BlockSpec((None, d, qd), lambda i: (layer, 0, 0)),
            pl.BlockSpec((None, N_KEYS, HALF), lambda i: (layer, 0, 0)),
            pl.BlockSpec((None, N_KEYS, HALF), lambda i: (layer, 0, 0)),
        ],
        out_specs=[pl.BlockSpec((d, tm), lambda i: (0, i)), sspec(), sspec()],
        out_shape=[jax.ShapeDtypeStruct((d, n), jnp.bfloat16), sshape, sshape],
        scratch_shapes=[pltpu.VMEM((tm, qd), jnp.float32)],
        compiler_params=_params(("parallel",)),
        name="peer_scores",
    )(h, g, wq, k1, k2)


def _top_values(x, k):
    out = []
    for _ in range(k):
        m = jnp.max(x, axis=0, keepdims=True)
        out.append(m)
        x = jnp.where(x == m, -jnp.inf, x)
    return out


def _peer_topk_kernel(s1_ref, s2_ref, thk_ref, e1k_ref, e2_ref):
    kk = TOPK + 1
    for h in range(PEER_HEADS):
        s1 = s1_ref[h]
        s2 = s2_ref[h]
        v1 = _top_values(s1, kk)
        v2 = jnp.concatenate(_top_values(s2, kk), axis=0)
        cand = [v1[a] + v2[0:kk // (a + 1), :] for a in range(kk)]
        n_cand = sum(c.shape[0] for c in cand)
        pad = -n_cand % SUBLANES
        if pad:
            cand.append(jnp.full((pad, LANES), -jnp.inf, jnp.float32))
        sc = _top_values(jnp.concatenate(cand, axis=0), kk)
        top = sc[0]
        z = jnp.zeros_like(top)
        for c in sc[:TOPK]:
            z = z + jnp.exp(c - top)
        tau = 0.5 * (sc[TOPK - 1] + sc[TOPK])
        thk_ref[:, h, :] = tau - s1
        e1k_ref[:, h, :] = jnp.exp(s1 - v1[0]) * (1.0 / z)
        e2_ref[h] = jnp.exp(s2 - v2[0:1, :])


def _peer_topk(s1, s2):
    n = s1.shape[-1]
    sspec = lambda: pl.BlockSpec((PEER_HEADS, N_KEYS, LANES), lambda i: (0, 0, i))
    kspec = lambda: pl.BlockSpec((N_KEYS, PEER_HEADS, LANES), lambda i: (0, 0, i))
    kshape = jax.ShapeDtypeStruct((N_KEYS, PEER_HEADS, n), jnp.float32)
    return pl.pallas_call(
        _peer_topk_kernel,
        grid=(n // LANES,),
        in_specs=[sspec(), sspec()],
        out_specs=[kspec(), kspec(), sspec()],
        out_shape=[kshape, kshape, jax.ShapeDtypeStruct(s1.shape, jnp.float32)],
        compiler_params=_params(("parallel",)),
        name="peer_topk",
    )(s1, s2)


def _peer_kernel(ut_ref, thp_ref, thc_ref, e1p_ref, e1c_ref, s2_ref, e2_ref, eu_ref,
                 evt_ref, o_ref, a0, a1, w0, w1, acc, *, tm, nb):
    j = pl.program_id(1)
    half = a0.shape[0]
    n_i1 = half // N_KEYS

    @pl.when(j == 0)
    def _():
        acc[...] = jnp.zeros_like(acc)
        w0[...] = jnp.zeros_like(w0)
        a1[...] = jnp.zeros_like(a1)

    def unit(e0, a_dst, a_src, w_dst, w_src, th_ref, e1_ref, row0, valid):
        eh = pl.ds(e0, half)
        a_dst[...] = jnp.dot(eu_ref[eh, :], ut_ref[...],
                             preferred_element_type=jnp.float32)
        for lg in range(tm // LANES):
            ls = slice(lg * LANES, (lg + 1) * LANES)
            for il in range(n_i1):
                r = row0 + il
                er = slice(il * N_KEYS, (il + 1) * N_KEYS)
                g = jnp.zeros((N_KEYS, LANES), jnp.float32)
                for h in range(PEER_HEADS):
                    e1 = jnp.where(valid, e1_ref[r, h:h + 1, ls], 0.0)
                    sel = s2_ref[h, :, ls] >= th_ref[r, h:h + 1, ls]
                    g = g + jnp.where(sel, e2_ref[h, :, ls] * e1, 0.0)
                a = a_src[er, ls]
                act = 0.5 * a * (1.0 + lax.erf(a * math.sqrt(0.5)))
                w_dst[er, ls] = (g * act).astype(w_dst.dtype)
        acc[...] += jnp.dot(evt_ref[:, eh], w_src[...],
                            preferred_element_type=jnp.float32)

    unit(0, a0, a1, w1, w0, thp_ref, e1p_ref, n_i1, j > 0)

    @pl.when(j >= 0)
    def _():
        unit(half, a1, a0, w0, w1, thc_ref, e1c_ref, 0, j < nb)

    @pl.when(j == nb)
    def _():
        o_ref[...] = acc[...].T


def _peer(ut, thk, s2, e1k, e2, eu, evt, layer, tm):
    d, n = ut.shape
    te = SUBLANES * N_KEYS
    nb = N_EXPERTS // te
    once = pl.Buffered(1)
    prev = lambda j: jnp.maximum(j - 1, 0)
    cur = lambda j: jnp.minimum(j, nb - 1)
    rows_p = lambda: pl.BlockSpec((SUBLANES, PEER_HEADS, tm), lambda i, j: (prev(j), 0, i))
    rows_c = lambda: pl.BlockSpec((SUBLANES, PEER_HEADS, tm), lambda i, j: (cur(j), 0, i))
    full = lambda: pl.BlockSpec((PEER_HEADS, N_KEYS, tm), lambda i, j: (0, 0, i),
                                pipeline_mode=once)
    return pl.pallas_call(
        functools.partial(_peer_kernel, tm=tm, nb=nb),
        grid=(n // tm, nb + 1),
        in_specs=[
            pl.BlockSpec((d, tm), lambda i, j: (0, i), pipeline_mode=once),
            rows_p(), rows_c(), rows_p(), rows_c(),
            full(), full(),
            pl.BlockSpec((None, te, d), lambda i, j: (layer, cur(j), 0)),
            pl.BlockSpec((None, d, te), lambda i, j: (layer, 0, prev(j))),
        ],
        out_specs=pl.BlockSpec((tm, d), lambda i, j: (i, 0), pipeline_mode=once),
        out_shape=jax.ShapeDtypeStruct((n, d), jnp.float32),
        scratch_shapes=[pltpu.VMEM((te // 2, tm), jnp.float32), pltpu.VMEM((te // 2, tm), jnp.float32),
                        pltpu.VMEM((te // 2, tm), jnp.bfloat16), pltpu.VMEM((te // 2, tm), jnp.bfloat16),
                        pltpu.VMEM((d, tm), jnp.float32)],
        compiler_params=_params(("parallel", "arbitrary")),
        name="peer_dense",
    )(ut, thk, thk, e1k, e1k, s2, e2, eu, evt)


def _ple_kernel(h_ref, dl_ref, p_ref, g_ref, wg_ref, wp_ref, o_ref, h2_scr, u_scr, *, tn):
    j = pl.program_id(1)

    @pl.when(j == 0)
    def _():
        h2 = h_ref[...] + dl_ref[...]
        h2_scr[...] = h2
        u_scr[...] = _rms(h2, g_ref[...]).astype(jnp.bfloat16)

    gate = _sigmoid(jnp.dot(u_scr[...], wg_ref[...], preferred_element_type=jnp.float32))
    emb = jnp.dot(p_ref[...].astype(jnp.bfloat16), wp_ref[...],
                  preferred_element_type=jnp.float32)
    c0 = pl.multiple_of(j * tn, tn)
    o_ref[...] = h2_scr[:, pl.ds(c0, tn)] + emb * gate


def _ple(h, delta, p, g, wg, wp, layer, tm, tn):
    n, d = h.shape
    pd = p.shape[-1]
    return pl.pallas_call(
        functools.partial(_ple_kernel, tn=tn),
        grid=(n // tm, d // tn),
        in_specs=[
            pl.BlockSpec((tm, d), lambda i, j: (i, 0)),
            pl.BlockSpec((tm, d), lambda i, j: (i, 0)),
            pl.BlockSpec((None, tm, pd), lambda i, j: (layer, i, 0)),
            pl.BlockSpec((None, 1, d), lambda i, j: (layer, 0, 0)),
            pl.BlockSpec((None, d, tn), lambda i, j: (layer, 0, j)),
            pl.BlockSpec((None, pd, tn), lambda i, j: (layer, 0, j)),
        ],
        out_specs=pl.BlockSpec((tm, tn), lambda i, j: (i, j)),
        out_shape=jax.ShapeDtypeStruct((n, d), jnp.float32),
        scratch_shapes=[pltpu.VMEM((tm, d), jnp.float32), pltpu.VMEM((tm, d), jnp.bfloat16)],
        compiler_params=_params(("parallel", "arbitrary")),
        name="ple",
    )(h, delta, p, g, wg, wp)


def _final_norm_kernel(h_ref, g_ref, o_ref):
    o_ref[...] = _rms(h_ref[...], g_ref[...])


def _final_norm(h, g, row0, rows):
    d = h.shape[1]
    tm = max(t for t in (4 * LANES, 2 * LANES, LANES) if rows % t == 0 and row0 % t == 0)
    b0 = row0 // tm
    return pl.pallas_call(
        _final_norm_kernel,
        grid=(rows // tm,),
        in_specs=[pl.BlockSpec((tm, d), lambda i: (b0 + i, 0)),
                  pl.BlockSpec((1, d), lambda i: (0, 0))],
        out_specs=pl.BlockSpec((tm, d), lambda i: (i, 0)),
        out_shape=jax.ShapeDtypeStruct((rows, d), jnp.float32),
        compiler_params=_params(("parallel",)),
        name="final_norm",
    )(h, g)


def kernel(x_prompt, x_sample, p_prompt, p_sample, state_conv_a, state_conv_b, norm1_g, w_in, conv_a_w, conv_a_b, ln_a_g, ln_a_b, conv_b_w, w_out, norm2_g, peer_wq, peer_k1, peer_k2, peer_u, peer_v, norm3_g, ple_w, ple_gate_w, final_g):
    batch, seq, d = x_prompt.shape
    dec = x_sample.shape[0]
    depth = w_in.shape[0]
    n_prompt = batch * seq
    n = n_prompt + dec
    assert d == D_MODEL and x_sample.shape[1] == 1
    assert seq % LANES == 0 and dec % LANES == 0 and seq >= A_HALO
    tp = PEER_TOKEN_TILE
    tm = tp // 2
    n_pad = -(-n // tp) * tp
    tb = LANES
    bc = 32
    ha = CONV_A_W - 1
    hb = CONV_B_W - 1
    bf16 = jnp.bfloat16

    pd = p_prompt.shape[-1]
    h = jnp.concatenate([x_prompt.reshape(n_prompt, d), x_sample.reshape(dec, d),
                         jnp.zeros((n_pad - n, d), x_prompt.dtype)], axis=0)
    p_all = jnp.concatenate([p_prompt.reshape(depth, n_prompt, pd),
                             p_sample.reshape(depth, dec, pd),
                             jnp.zeros((depth, n_pad - n, pd), p_prompt.dtype)], axis=1)
    sa2 = state_conv_a.reshape(depth, dec, ha * WIDTH)
    sb2 = state_conv_b.reshape(depth, dec, hb * WIDTH)
    row = lambda v: v.reshape(depth, 1, v.shape[-1])
    w_in_b, w_out_b, wq_b = w_in.astype(bf16), w_out.astype(bf16), peer_wq.astype(bf16)
    eu_b, evt_b = peer_u.astype(bf16), jnp.swapaxes(peer_v, 1, 2).astype(bf16)
    wg_b, wp_b = ple_gate_w.astype(bf16), ple_w.astype(bf16)
    n1, n2, n3 = row(norm1_g), row(norm2_g), row(norm3_g)
    ba, lg, lb = row(conv_a_b), row(ln_a_g), row(ln_a_b)

    na_p, na_s, nb_p, nb_s = [], [], [], []
    for l in range(depth):
        proj = _norm_matmul(h, n1, w_in_b, l, tp, PROJ_IN // 4)
        mix, ta, tbb = _conv_prompt(proj, conv_a_w, ba, lg, lb, conv_b_w, l, batch, seq, tb)
        mix, sa_new, sb_new = _conv_sample(proj, sa2, sb2, conv_a_w, ba, lg, lb, conv_b_w,
                                           mix, l, n_prompt, bc)
        na_p.append(ta[:, A_HALO - ha:, :])
        nb_p.append(tbb[:, B_HALO - hb:, :])
        na_s.append(sa_new.reshape(dec, ha, WIDTH))
        nb_s.append(sb_new.reshape(dec, hb, WIDTH))
        h1 = _matmul_res(mix, w_out_b, h, l, tp, d // 2)
        ut, s1, s2 = _peer_scores(h1, n2, wq_b, peer_k1, peer_k2, l, tm)
        thk, e1k, e2 = _peer_topk(s1, s2)
        delta = _peer(ut, thk, s2, e1k, e2, eu_b, evt_b, l, tp)
        h = _ple(h1, delta, p_all, n3, wg_b, wp_b, l, tm, d // 2)

    fg = final_g.reshape(1, d)
    y_prompt = _final_norm(h, fg, 0, n_prompt).reshape(batch, seq, d)
    y_sample = _final_norm(h, fg, n_prompt, dec).reshape(dec, 1, d)
    return (y_prompt, y_sample, jnp.stack(na_p), jnp.stack(na_s), jnp.stack(nb_p), jnp.stack(nb_s))
```

```python
import functools
import math

import jax
import jax.numpy as jnp
from jax import lax
from jax.experimental import pallas as pl
from jax.experimental.pallas import tpu as pltpu

D_MODEL = 2048
WIDTH = D_MODEL // 2
CONV_A_W = 31
CONV_B_W = 3
PROJ_IN = 5 * WIDTH
PEER_HEADS = 8
N_KEYS = 128
N_EXPERTS = N_KEYS * N_KEYS
HALF = 128
PEER_QDIM = 2 * HALF
TOPK = 16
RMS_EPS = 1e-6
LN_EPS = 1e-5

LANES = 128
SUBLANES = 8
A_HALO = 32
B_HALO = 8
VMEM_LIMIT = 56 * 1024 * 1024

_NT = (((1,), (1,)), ((), ()))
_TN = (((0,), (0,)), ((), ()))


def _params(sem):
    return pltpu.CompilerParams(dimension_semantics=sem, vmem_limit_bytes=VMEM_LIMIT)


MXU_TILE = 256
PEER_TOKEN_TILE = 3 * MXU_TILE


def _rms(x, g):
    ms = jnp.mean(x * x, axis=-1, keepdims=True)
    return x * lax.rsqrt(ms + RMS_EPS) * g


def _sigmoid(x):
    return 1.0 / (1.0 + jnp.exp(-x))


def _norm_matmul_kernel(x_ref, g_ref, w_ref, o_ref, u_scr):
    @pl.when(pl.program_id(1) == 0)
    def _():
        u_scr[...] = _rms(x_ref[...], g_ref[...]).astype(jnp.bfloat16)

    o_ref[...] = jnp.dot(u_scr[...], w_ref[...], preferred_element_type=jnp.float32)


def _norm_matmul(x, g, w, layer, tm, tn):
    n, d = x.shape
    e = w.shape[-1]
    return pl.pallas_call(
        _norm_matmul_kernel,
        grid=(n // tm, e // tn),
        in_specs=[
            pl.BlockSpec((tm, d), lambda i, j: (i, 0)),
            pl.BlockSpec((None, 1, d), lambda i, j: (layer, 0, 0)),
            pl.BlockSpec((None, d, tn), lambda i, j: (layer, 0, j)),
        ],
        out_specs=pl.BlockSpec((tm, tn), lambda i, j: (i, j)),
        out_shape=jax.ShapeDtypeStruct((n, e), jnp.float32),
        scratch_shapes=[pltpu.VMEM((tm, d), jnp.bfloat16)],
        compiler_params=_params(("parallel", "arbitrary")),
        name="norm_matmul",
    )(x, g, w)


def _ln_silu(y, g, b):
    mu = jnp.mean(y, axis=-1, keepdims=True)
    yc = y - mu
    var = jnp.mean(yc * yc, axis=-1, keepdims=True)
    ln = yc * lax.rsqrt(var + LN_EPS) * g + b
    return ln * _sigmoid(ln)


def _conv_prompt_kernel(p_ref, wa_ref, ba_ref, lg_ref, lb_ref, wb_ref,
                        mix_ref, na_ref, nb_ref, abuf, bbuf, ybuf, ashift, *, tb, cw, nt, n_real):
    s = pl.program_id(0)

    @pl.when(s >= n_real)
    def _():
        mix_ref[...] = jnp.zeros_like(mix_ref)

    @pl.when(s < n_real)
    def _():
        _conv_prompt_block(p_ref, wa_ref, ba_ref, lg_ref, lb_ref, wb_ref, mix_ref, na_ref, nb_ref,
                           abuf, bbuf, ybuf, ashift, s % nt, tb, cw)


def _conv_prompt_block(p_ref, wa_ref, ba_ref, lg_ref, lb_ref, wb_ref, mix_ref, na_ref, nb_ref,
                       abuf, bbuf, ybuf, ashift, t, tb, cw):
    @pl.when(t == 0)
    def _():
        abuf[0:A_HALO, :] = jnp.zeros((A_HALO, WIDTH), jnp.float32)
        bbuf[0:B_HALO, :] = jnp.zeros((B_HALO, WIDTH), jnp.float32)

    @pl.when(t > 0)
    def _():
        abuf[0:A_HALO, :] = abuf[tb:tb + A_HALO, :]
        bbuf[0:B_HALO, :] = bbuf[tb:tb + B_HALO, :]

    a_glu = p_ref[:, 0:WIDTH] * _sigmoid(p_ref[:, WIDTH:2 * WIDTH])
    abuf[A_HALO:A_HALO + tb, :] = a_glu
    n_sh = ashift.shape[1]
    for r in range(1, SUBLANES):
        ashift[r - 1, :, :] = abuf[r:r + n_sh, :]
    off_a = A_HALO - (CONV_A_W - 1)
    for c0 in range(0, WIDTH, cw):
        acc = jnp.zeros((tb, cw), jnp.float32)
        for j in range(CONV_A_W):
            q, r = divmod(off_a + j, SUBLANES)
            q *= SUBLANES
            if r == 0:
                rows = abuf[q:q + tb, c0:c0 + cw]
            else:
                rows = ashift[r - 1, q:q + tb, c0:c0 + cw]
            acc = acc + wa_ref[j:j + 1, c0:c0 + cw] * rows
        ybuf[:, c0:c0 + cw] = acc + ba_ref[:, c0:c0 + cw]
    a = _ln_silu(ybuf[...], lg_ref[...], lb_ref[...])
    mix_ref[:, 0:WIDTH] = a.astype(mix_ref.dtype)

    cb = p_ref[:, 3 * WIDTH:4 * WIDTH] * p_ref[:, 4 * WIDTH:5 * WIDTH]
    bbuf[B_HALO:B_HALO + tb, :] = cb
    off_b = B_HALO - (CONV_B_W - 1)
    z = jnp.zeros((tb, WIDTH), jnp.float32)
    for j in range(CONV_B_W):
        z = z + wb_ref[j:j + 1, :] * bbuf[off_b + j:off_b + j + tb, :]
    mix_ref[:, WIDTH:2 * WIDTH] = (p_ref[:, 2 * WIDTH:3 * WIDTH] * z).astype(mix_ref.dtype)

    na_ref[...] = abuf[tb:tb + A_HALO, :]
    nb_ref[...] = bbuf[tb:tb + B_HALO, :]


def _conv_prompt(proj, wa, ba, lg, lb, wb, layer, batch, seq, tb):
    n = proj.shape[0]
    nt = seq // tb
    n_real = batch * nt
    seq_of = lambda s: jnp.minimum(s // nt, batch - 1)
    vec = lambda: pl.BlockSpec((None, 1, WIDTH), lambda s: (layer, 0, 0))
    return pl.pallas_call(
        functools.partial(_conv_prompt_kernel, tb=tb, cw=2 * LANES, nt=nt, n_real=n_real),
        grid=(n // tb,),
        in_specs=[
            pl.BlockSpec((tb, PROJ_IN), lambda s: (s, 0)),
            pl.BlockSpec((None, CONV_A_W, WIDTH), lambda s: (layer, 0, 0)),
            vec(), vec(), vec(),
            pl.BlockSpec((None, CONV_B_W, WIDTH), lambda s: (layer, 0, 0)),
        ],
        out_specs=[
            pl.BlockSpec((tb, D_MODEL), lambda s: (s, 0)),
            pl.BlockSpec((None, A_HALO, WIDTH), lambda s: (seq_of(s), 0, 0)),
            pl.BlockSpec((None, B_HALO, WIDTH), lambda s: (seq_of(s), 0, 0)),
        ],
        out_shape=[
            jax.ShapeDtypeStruct((n, D_MODEL), jnp.bfloat16),
            jax.ShapeDtypeStruct((batch, A_HALO, WIDTH), jnp.float32),
            jax.ShapeDtypeStruct((batch, B_HALO, WIDTH), jnp.float32),
        ],
        scratch_shapes=[
            pltpu.VMEM((A_HALO + tb, WIDTH), jnp.float32),
            pltpu.VMEM((B_HALO + tb, WIDTH), jnp.float32),
            pltpu.VMEM((tb, WIDTH), jnp.float32),
            pltpu.VMEM((SUBLANES - 1, A_HALO + tb - SUBLANES, WIDTH), jnp.float32),
        ],
        compiler_params=_params(("arbitrary",)),
        name="conv_prompt",
    )(proj, wa, ba, lg, lb, wb)


def _conv_sample_kernel(p_ref, sa_ref, sb_ref, wa_ref, ba_ref, lg_ref, lb_ref, wb_ref, mix_in,
                        mix_ref, na_ref, nb_ref):
    del mix_in
    ha = CONV_A_W - 1
    hb = CONV_B_W - 1
    a_glu = p_ref[:, 0:WIDTH] * _sigmoid(p_ref[:, WIDTH:2 * WIDTH])
    acc = wa_ref[ha:ha + 1, :] * a_glu
    for j in range(ha):
        acc = acc + wa_ref[j:j + 1, :] * sa_ref[:, j * WIDTH:(j + 1) * WIDTH]
    a = _ln_silu(acc + ba_ref[...], lg_ref[...], lb_ref[...])
    mix_ref[:, 0:WIDTH] = a.astype(mix_ref.dtype)
    na_ref[:, 0:(ha - 1) * WIDTH] = sa_ref[:, WIDTH:ha * WIDTH]
    na_ref[:, (ha - 1) * WIDTH:ha * WIDTH] = a_glu

    cb = p_ref[:, 3 * WIDTH:4 * WIDTH] * p_ref[:, 4 * WIDTH:5 * WIDTH]
    z = wb_ref[hb:hb + 1, :] * cb
    for j in range(hb):
        z = z + wb_ref[j:j + 1, :] * sb_ref[:, j * WIDTH:(j + 1) * WIDTH]
    mix_ref[:, WIDTH:2 * WIDTH] = (p_ref[:, 2 * WIDTH:3 * WIDTH] * z).astype(mix_ref.dtype)
    nb_ref[:, 0:(hb - 1) * WIDTH] = sb_ref[:, WIDTH:hb * WIDTH]
    nb_ref[:, (hb - 1) * WIDTH:hb * WIDTH] = cb


def _conv_sample(proj, sa2, sb2, wa, ba, lg, lb, wb, mix, layer, n_prompt, bc):
    dec = sa2.shape[1]
    ha = CONV_A_W - 1
    hb = CONV_B_W - 1
    r0 = n_prompt // bc
    vec = lambda: pl.BlockSpec((None, 1, WIDTH), lambda i: (layer, 0, 0))
    return pl.pallas_call(
        _conv_sample_kernel,
        grid=(dec // bc,),
        in_specs=[
            pl.BlockSpec((bc, PROJ_IN), lambda i: (r0 + i, 0)),
            pl.BlockSpec((None, bc, ha * WIDTH), lambda i: (layer, i, 0)),
            pl.BlockSpec((None, bc, hb * WIDTH), lambda i: (layer, i, 0)),
            pl.BlockSpec((None, CONV_A_W, WIDTH), lambda i: (layer, 0, 0)),
            vec(), vec(), vec(),
            pl.BlockSpec((None, CONV_B_W, WIDTH), lambda i: (layer, 0, 0)),
            pl.BlockSpec(memory_space=pl.ANY),
        ],
        out_specs=[
            pl.BlockSpec((bc, D_MODEL), lambda i: (r0 + i, 0)),
            pl.BlockSpec((bc, ha * WIDTH), lambda i: (i, 0)),
            pl.BlockSpec((bc, hb * WIDTH), lambda i: (i, 0)),
        ],
        out_shape=[
            jax.ShapeDtypeStruct(mix.shape, mix.dtype),
            jax.ShapeDtypeStruct((dec, ha * WIDTH), jnp.float32),
            jax.ShapeDtypeStruct((dec, hb * WIDTH), jnp.float32),
        ],
        input_output_aliases={8: 0},
        compiler_params=_params(("parallel",)),
        name="conv_sample",
    )(proj, sa2, sb2, wa, ba, lg, lb, wb, mix)


def _matmul_res_kernel(m_ref, w_ref, h_ref, o_ref):
    o_ref[...] = h_ref[...] + jnp.dot(m_ref[...], w_ref[...],
                                      preferred_element_type=jnp.float32)


def _matmul_res(mix, w, h, layer, tm, tn):
    n, k = mix.shape
    d = w.shape[-1]
    return pl.pallas_call(
        _matmul_res_kernel,
        grid=(n // tm, d // tn),
        in_specs=[
            pl.BlockSpec((tm, k), lambda i, j: (i, 0)),
            pl.BlockSpec((None, k, tn), lambda i, j: (layer, 0, j)),
            pl.BlockSpec((tm, tn), lambda i, j: (i, j)),
        ],
        out_specs=pl.BlockSpec((tm, tn), lambda i, j: (i, j)),
        out_shape=jax.ShapeDtypeStruct((n, d), jnp.float32),
        compiler_params=_params(("parallel", "arbitrary")),
        name="matmul_res",
    )(mix, w, h)


def _peer_scores_kernel(h_ref, g_ref, wq_ref, k1_ref, k2_ref, ut_ref, s1_ref, s2_ref, q_scr):
    u32 = _rms(h_ref[...], g_ref[...])
    ut_ref[...] = u32.T.astype(jnp.bfloat16)
    q_scr[...] = jnp.dot(u32.astype(jnp.bfloat16), wq_ref[...],
                         preferred_element_type=jnp.float32)
    k1 = k1_ref[...].astype(jnp.bfloat16)
    k2 = k2_ref[...].astype(jnp.bfloat16)
    for h in range(PEER_HEADS):
        c0 = h * PEER_QDIM
        q1 = q_scr[:, c0:c0 + HALF].astype(jnp.bfloat16)
        q2 = q_scr[:, c0 + HALF:c0 + PEER_QDIM].astype(jnp.bfloat16)
        s1_ref[h] = lax.dot_general(k1, q1, _NT, preferred_element_type=jnp.float32)
        s2_ref[h] = lax.dot_general(k2, q2, _NT, preferred_element_type=jnp.float32)


def _peer_scores(h, g, wq, k1, k2, layer, tm):
    n, d = h.shape
    qd = wq.shape[-1]
    sshape = jax.ShapeDtypeStruct((PEER_HEADS, N_KEYS, n), jnp.float32)
    sspec = lambda: pl.BlockSpec((PEER_HEADS, N_KEYS, tm), lambda i: (0, 0, i))
    return pl.pallas_call(
        _peer_scores_kernel,
        grid=(n // tm,),
        in_specs=[
            pl.BlockSpec((tm, d), lambda i: (i, 0)),
            pl.BlockSpec((None, 1, d), lambda i: (layer, 0, 0)),
            pl.BlockSpec((None, d, qd), lambda i: (layer, 0, 0)),
            pl.BlockSpec((None, N_KEYS, HALF), lambda i: (layer, 0, 0)),
            pl.BlockSpec((None, N_KEYS, HALF), lambda i: (layer, 0, 0)),
        ],
        out_specs=[pl.BlockSpec((d, tm), lambda i: (0, i)), sspec(), sspec()],
        out_shape=[jax.ShapeDtypeStruct((d, n), jnp.bfloat16), sshape, sshape],
        scratch_shapes=[pltpu.VMEM((tm, qd), jnp.float32)],
        compiler_params=_params(("parallel",)),
        name="peer_scores",
    )(h, g, wq, k1, k2)


def _top_values(x, k):
    out = []
    for _ in range(k):
        m = jnp.max(x, axis=0, keepdims=True)
        out.append(m)
        x = jnp.where(x == m, -jnp.inf, x)
    return out


def _peer_topk_kernel(s1_ref, s2_ref, thk_ref, e1k_ref, e2_ref):
    kk = TOPK + 1
    for h in range(PEER_HEADS):
        s1 = s1_ref[h]
        s2 = s2_ref[h]
        v1 = _top_values(s1, kk)
        v2 = jnp.concatenate(_top_values(s2, kk), axis=0)
        cand = [v1[a] + v2[0:kk // (a + 1), :] for a in range(kk)]
        n_cand = sum(c.shape[0] for c in cand)
        pad = -n_cand % SUBLANES
        if pad:
            cand.append(jnp.full((pad, LANES), -jnp.inf, jnp.float32))
        sc = _top_values(jnp.concatenate(cand, axis=0), kk)
        top = sc[0]
        z = jnp.zeros_like(top)
        for c in sc[:TOPK]:
            z = z + jnp.exp(c - top)
        tau = 0.5 * (sc[TOPK - 1] + sc[TOPK])
        thk_ref[:, h, :] = tau - s1
        e1k_ref[:, h, :] = jnp.exp(s1 - v1[0]) * (1.0 / z)
        e2_ref[h] = jnp.exp(s2 - v2[0:1, :])


def _peer_topk(s1, s2):
    n = s1.shape[-1]
    sspec = lambda: pl.BlockSpec((PEER_HEADS, N_KEYS, LANES), lambda i: (0, 0, i))
    kspec = lambda: pl.BlockSpec((N_KEYS, PEER_HEADS, LANES), lambda i: (0, 0, i))
    kshape = jax.ShapeDtypeStruct((N_KEYS, PEER_HEADS, n), jnp.float32)
    return pl.pallas_call(
        _peer_topk_kernel,
        grid=(n // LANES,),
        in_specs=[sspec(), sspec()],
        out_specs=[kspec(), kspec(), sspec()],
        out_shape=[kshape, kshape, jax.ShapeDtypeStruct(s1.shape, jnp.float32)],
        compiler_params=_params(("parallel",)),
        name="peer_topk",
    )(s1, s2)


def _peer_kernel(ut_ref, th_ref, e1_ref, s2_ref, e2_ref, eu_ref, evt_ref, o_ref, a_scr, w_scr, acc,
                 *, tm, nb):
    j = pl.program_id(1)
    n_i1 = a_scr.shape[0] // N_KEYS

    @pl.when(j == 0)
    def _():
        acc[...] = jnp.zeros_like(acc)

    a_scr[...] = jnp.dot(eu_ref[...], ut_ref[...], preferred_element_type=jnp.float32)

    @pl.when(j >= 0)
    def _():
        for lg in range(tm // LANES):
            ls = slice(lg * LANES, (lg + 1) * LANES)
            for il in range(n_i1):
                er = slice(il * N_KEYS, (il + 1) * N_KEYS)
                g = jnp.zeros((N_KEYS, LANES), jnp.float32)
                for h in range(PEER_HEADS):
                    sel = s2_ref[h, :, ls] >= th_ref[il, h:h + 1, ls]
                    g = g + jnp.where(sel, e2_ref[h, :, ls] * e1_ref[il, h:h + 1, ls], 0.0)
                a = a_scr[er, ls]
                act = 0.5 * a * (1.0 + lax.erf(a * math.sqrt(0.5)))
                w_scr[er, ls] = (g * act).astype(w_scr.dtype)

    @pl.when(j >= 0)
    def _():
        acc[...] += jnp.dot(evt_ref[...], w_scr[...], preferred_element_type=jnp.float32)

    @pl.when(j == nb - 1)
    def _():
        o_ref[...] = acc[...].T


def _peer(ut, thk, s2, e1k, e2, eu, evt, layer, tm):
    d, n = ut.shape
    te = SUBLANES * N_KEYS
    nb = N_EXPERTS // te
    once = pl.Buffered(1)
    rows = lambda: pl.BlockSpec((SUBLANES, PEER_HEADS, tm), lambda i, j: (j, 0, i))
    full = lambda: pl.BlockSpec((PEER_HEADS, N_KEYS, tm), lambda i, j: (0, 0, i),
                                pipeline_mode=once)
    return pl.pallas_call(
        functools.partial(_peer_kernel, tm=tm, nb=nb),
        grid=(n // tm, nb),
        in_specs=[
            pl.BlockSpec((d, tm), lambda i, j: (0, i), pipeline_mode=once),
            rows(), rows(), full(), full(),
            pl.BlockSpec((None, te, d), lambda i, j: (layer, j, 0)),
            pl.BlockSpec((None, d, te), lambda i, j: (layer, 0, j)),
        ],
        out_specs=pl.BlockSpec((tm, d), lambda i, j: (i, 0), pipeline_mode=once),
        out_shape=jax.ShapeDtypeStruct((n, d), jnp.float32),
        scratch_shapes=[pltpu.VMEM((te, tm), jnp.float32), pltpu.VMEM((te, tm), jnp.bfloat16),
                        pltpu.VMEM((d, tm), jnp.float32)],
        compiler_params=_params(("parallel", "arbitrary")),
        name="peer_dense",
    )(ut, thk, e1k, s2, e2, eu, evt)


def _ple_kernel(h_ref, dl_ref, p_ref, g_ref, wg_ref, wp_ref, o_ref, h2_scr, u_scr, *, tn):
    j = pl.program_id(1)

    @pl.when(j == 0)
    def _():
        h2 = h_ref[...] + dl_ref[...]
        h2_scr[...] = h2
        u_scr[...] = _rms(h2, g_ref[...]).astype(jnp.bfloat16)

    gate = _sigmoid(jnp.dot(u_scr[...], wg_ref[...], preferred_element_type=jnp.float32))
    emb = jnp.dot(p_ref[...].astype(jnp.bfloat16), wp_ref[...],
                  preferred_element_type=jnp.float32)
    c0 = pl.multiple_of(j * tn, tn)
    o_ref[...] = h2_scr[:, pl.ds(c0, tn)] + emb * gate


def _ple(h, delta, p, g, wg, wp, layer, tm, tn):
    n, d = h.shape
    pd = p.shape[-1]
    return pl.pallas_call(
        functools.partial(_ple_kernel, tn=tn),
        grid=(n // tm, d // tn),
        in_specs=[
            pl.BlockSpec((tm, d), lambda i, j: (i, 0)),
            pl.BlockSpec((tm, d), lambda i, j: (i, 0)),
            pl.BlockSpec((None, tm, pd), lambda i, j: (layer, i, 0)),
            pl.BlockSpec((None, 1, d), lambda i, j: (layer, 0, 0)),
            pl.BlockSpec((None, d, tn), lambda i, j: (layer, 0, j)),
            pl.BlockSpec((None, pd, tn), lambda i, j: (layer, 0, j)),
        ],
        out_specs=pl.BlockSpec((tm, tn), lambda i, j: (i, j)),
        out_shape=jax.ShapeDtypeStruct((n, d), jnp.float32),
        scratch_shapes=[pltpu.VMEM((tm, d), jnp.float32), pltpu.VMEM((tm, d), jnp.bfloat16)],
        compiler_params=_params(("parallel", "arbitrary")),
        name="ple",
    )(h, delta, p, g, wg, wp)


def _final_norm_kernel(h_ref, g_ref, o_ref):
    o_ref[...] = _rms(h_ref[...], g_ref[...])


def _final_norm(h, g, row0, rows):
    d = h.shape[1]
    tm = max(t for t in (4 * LANES, 2 * LANES, LANES) if rows % t == 0 and row0 % t == 0)
    b0 = row0 // tm
    return pl.pallas_call(
        _final_norm_kernel,
        grid=(rows // tm,),
        in_specs=[pl.BlockSpec((tm, d), lambda i: (b0 + i, 0)),
                  pl.BlockSpec((1, d), lambda i: (0, 0))],
        out_specs=pl.BlockSpec((tm, d), lambda i: (i, 0)),
        out_shape=jax.ShapeDtypeStruct((rows, d), jnp.float32),
        compiler_params=_params(("parallel",)),
        name="final_norm",
    )(h, g)


def kernel(x_prompt, x_sample, p_prompt, p_sample, state_conv_a, state_conv_b, norm1_g, w_in, conv_a_w, conv_a_b, ln_a_g, ln_a_b, conv_b_w, w_out, norm2_g, peer_wq, peer_k1, peer_k2, peer_u, peer_v, norm3_g, ple_w, ple_gate_w, final_g):
    batch, seq, d = x_prompt.shape
    dec = x_sample.shape[0]
    depth = w_in.shape[0]
    n_prompt = batch * seq
    n = n_prompt + dec
    assert d == D_MODEL and x_sample.shape[1] == 1
    assert seq % LANES == 0 and dec % LANES == 0 and seq >= A_HALO
    tp = PEER_TOKEN_TILE
    tm = tp // 2
    n_pad = -(-n // tp) * tp
    tb = LANES
    bc = 32
    ha = CONV_A_W - 1
    hb = CONV_B_W - 1
    bf16 = jnp.bfloat16

    pd = p_prompt.shape[-1]
    h = jnp.concatenate([x_prompt.reshape(n_prompt, d), x_sample.reshape(dec, d),
                         jnp.zeros((n_pad - n, d), x_prompt.dtype)], axis=0)
    p_all = jnp.concatenate([p_prompt.reshape(depth, n_prompt, pd),
                             p_sample.reshape(depth, dec, pd),
                             jnp.zeros((depth, n_pad - n, pd), p_prompt.dtype)], axis=1)
    sa2 = state_conv_a.reshape(depth, dec, ha * WIDTH)
    sb2 = state_conv_b.reshape(depth, dec, hb * WIDTH)
    row = lambda v: v.reshape(depth, 1, v.shape[-1])
    w_in_b, w_out_b, wq_b = w_in.astype(bf16), w_out.astype(bf16), peer_wq.astype(bf16)
    eu_b, evt_b = peer_u.astype(bf16), jnp.swapaxes(peer_v, 1, 2).astype(bf16)
    wg_b, wp_b = ple_gate_w.astype(bf16), ple_w.astype(bf16)
    n1, n2, n3 = row(norm1_g), row(norm2_g), row(norm3_g)
    ba, lg, lb = row(conv_a_b), row(ln_a_g), row(ln_a_b)

    na_p, na_s, nb_p, nb_s = [], [], [], []
    for l in range(depth):
        proj = _norm_matmul(h, n1, w_in_b, l, tp, PROJ_IN // 4)
        mix, ta, tbb = _conv_prompt(proj, conv_a_w, ba, lg, lb, conv_b_w, l, batch, seq, tb)
        mix, sa_new, sb_new = _conv_sample(proj, sa2, sb2, conv_a_w, ba, lg, lb, conv_b_w,
                                           mix, l, n_prompt, bc)
        na_p.append(ta[:, A_HALO - ha:, :])
        nb_p.append(tbb[:, B_HALO - hb:, :])
        na_s.append(sa_new.reshape(dec, ha, WIDTH))
        nb_s.append(sb_new.reshape(dec, hb, WIDTH))
        h1 = _matmul_res(mix, w_out_b, h, l, tp, d // 2)
        ut, s1, s2 = _peer_scores(h1, n2, wq_b, peer_k1, peer_k2, l, tm)
        thk, e1k, e2 = _peer_topk(s1, s2)
        delta = _peer(ut, thk, s2, e1k, e2, eu_b, evt_b, l, tp)
        h = _ple(h1, delta, p_all, n3, wg_b, wp_b, l, tm, d // 2)

    fg = final_g.reshape(1, d)
    y_prompt = _final_norm(h, fg, 0, n_prompt).reshape(batch, seq, d)
    y_sample = _final_norm(h, fg, n_prompt, dec).reshape(dec, 1, d)
    return (y_prompt, y_sample, jnp.stack(na_p), jnp.stack(na_s), jnp.stack(nb_p), jnp.stack(nb_s))
```

```python
import functools
import math

import jax
import jax.numpy as jnp
from jax import lax
from jax.experimental import pallas as pl
from jax.experimental.pallas import tpu as pltpu

D_MODEL = 2048
WIDTH = D_MODEL // 2
CONV_A_W = 31
CONV_B_W = 3
PROJ_IN = 5 * WIDTH
PEER_HEADS = 8
N_KEYS = 128
N_EXPERTS = N_KEYS * N_KEYS
HALF = 128
PEER_QDIM = 2 * HALF
TOPK = 16
RMS_EPS = 1e-6
LN_EPS = 1e-5

LANES = 128
SUBLANES = 8
A_HALO = 32
B_HALO = 8
VMEM_LIMIT = 56 * 1024 * 1024

_NT = (((1,), (1,)), ((), ()))
_TN = (((0,), (0,)), ((), ()))


def _params(sem):
    return pltpu.CompilerParams(dimension_semantics=sem, vmem_limit_bytes=VMEM_LIMIT)


MXU_TILE = 256
PEER_TOKEN_TILE = 3 * MXU_TILE


def _rms(x, g):
    ms = jnp.mean(x * x, axis=-1, keepdims=True)
    return x * lax.rsqrt(ms + RMS_EPS) * g


def _sigmoid(x):
    return 1.0 / (1.0 + jnp.exp(-x))


def _norm_matmul_kernel(x_ref, g_ref, w_ref, o_ref, u_scr):
    @pl.when(pl.program_id(1) == 0)
    def _():
        u_scr[...] = _rms(x_ref[...], g_ref[...]).astype(jnp.bfloat16)

    o_ref[...] = jnp.dot(u_scr[...], w_ref[...], preferred_element_type=jnp.float32)


def _norm_matmul(x, g, w, layer, tm, tn):
    n, d = x.shape
    e = w.shape[-1]
    return pl.pallas_call(
        _norm_matmul_kernel,
        grid=(n // tm, e // tn),
        in_specs=[
            pl.BlockSpec((tm, d), lambda i, j: (i, 0)),
            pl.BlockSpec((None, 1, d), lambda i, j: (layer, 0, 0)),
            pl.BlockSpec((None, d, tn), lambda i, j: (layer, 0, j)),
        ],
        out_specs=pl.BlockSpec((tm, tn), lambda i, j: (i, j)),
        out_shape=jax.ShapeDtypeStruct((n, e), jnp.float32),
        scratch_shapes=[pltpu.VMEM((tm, d), jnp.bfloat16)],
        compiler_params=_params(("parallel", "arbitrary")),
        name="norm_matmul",
    )(x, g, w)


def _ln_silu(y, g, b):
    mu = jnp.mean(y, axis=-1, keepdims=True)
    yc = y - mu
    var = jnp.mean(yc * yc, axis=-1, keepdims=True)
    ln = yc * lax.rsqrt(var + LN_EPS) * g + b
    return ln * _sigmoid(ln)


def _conv_prompt_kernel(p_ref, wa_ref, ba_ref, lg_ref, lb_ref, wb_ref,
                        mix_ref, na_ref, nb_ref, abuf, bbuf, ybuf, ashift, *, tb, cw, nt, n_real):
    s = pl.program_id(0)

    @pl.when(s >= n_real)
    def _():
        mix_ref[...] = jnp.zeros_like(mix_ref)

    @pl.when(s < n_real)
    def _():
        _conv_prompt_block(p_ref, wa_ref, ba_ref, lg_ref, lb_ref, wb_ref, mix_ref, na_ref, nb_ref,
                           abuf, bbuf, ybuf, ashift, s % nt, tb, cw)


def _conv_prompt_block(p_ref, wa_ref, ba_ref, lg_ref, lb_ref, wb_ref, mix_ref, na_ref, nb_ref,
                       abuf, bbuf, ybuf, ashift, t, tb, cw):
    @pl.when(t == 0)
    def _():
        abuf[0:A_HALO, :] = jnp.zeros((A_HALO, WIDTH), jnp.float32)
        bbuf[0:B_HALO, :] = jnp.zeros((B_HALO, WIDTH), jnp.float32)

    @pl.when(t > 0)
    def _():
        abuf[0:A_HALO, :] = abuf[tb:tb + A_HALO, :]
        bbuf[0:B_HALO, :] = bbuf[tb:tb + B_HALO, :]

    a_glu = p_ref[:, 0:WIDTH] * _sigmoid(p_ref[:, WIDTH:2 * WIDTH])
    abuf[A_HALO:A_HALO + tb, :] = a_glu
    n_sh = ashift.shape[1]
    for r in range(1, SUBLANES):
        ashift[r - 1, :, :] = abuf[r:r + n_sh, :]
    off_a = A_HALO - (CONV_A_W - 1)
    for c0 in range(0, WIDTH, cw):
        acc = jnp.zeros((tb, cw), jnp.float32)
        for j in range(CONV_A_W):
            q, r = divmod(off_a + j, SUBLANES)
            q *= SUBLANES
            if r == 0:
                rows = abuf[q:q + tb, c0:c0 + cw]
            else:
                rows = ashift[r - 1, q:q + tb, c0:c0 + cw]
            acc = acc + wa_ref[j:j + 1, c0:c0 + cw] * rows
        ybuf[:, c0:c0 + cw] = acc + ba_ref[:, c0:c0 + cw]
    a = _ln_silu(ybuf[...], lg_ref[...], lb_ref[...])
    mix_ref[:, 0:WIDTH] = a.astype(mix_ref.dtype)

    cb = p_ref[:, 3 * WIDTH:4 * WIDTH] * p_ref[:, 4 * WIDTH:5 * WIDTH]
    bbuf[B_HALO:B_HALO + tb, :] = cb
    off_b = B_HALO - (CONV_B_W - 1)
    z = jnp.zeros((tb, WIDTH), jnp.float32)
    for j in range(CONV_B_W):
        z = z + wb_ref[j:j + 1, :] * bbuf[off_b + j:off_b + j + tb, :]
    mix_ref[:, WIDTH:2 * WIDTH] = (p_ref[:, 2 * WIDTH:3 * WIDTH] * z).astype(mix_ref.dtype)

    na_ref[...] = abuf[tb:tb + A_HALO, :]
    nb_ref[...] = bbuf[tb:tb + B_HALO, :]


def _conv_prompt(proj, wa, ba, lg, lb, wb, layer, batch, seq, tb):
    n = proj.shape[0]
    nt = seq // tb
    n_real = batch * nt
    seq_of = lambda s: jnp.minimum(s // nt, batch - 1)
    vec = lambda: pl.BlockSpec((None, 1, WIDTH), lambda s: (layer, 0, 0))
    return pl.pallas_call(
        functools.partial(_conv_prompt_kernel, tb=tb, cw=2 * LANES, nt=nt, n_real=n_real),
        grid=(n // tb,),
        in_specs=[
            pl.BlockSpec((tb, PROJ_IN), lambda s: (s, 0)),
            pl.BlockSpec((None, CONV_A_W, WIDTH), lambda s: (layer, 0, 0)),
            vec(), vec(), vec(),
            pl.BlockSpec((None, CONV_B_W, WIDTH), lambda s: (layer, 0, 0)),
        ],
        out_specs=[
            pl.BlockSpec((tb, D_MODEL), lambda s: (s, 0)),
            pl.BlockSpec((None, A_HALO, WIDTH), lambda s: (seq_of(s), 0, 0)),
            pl.BlockSpec((None, B_HALO, WIDTH), lambda s: (seq_of(s), 0, 0)),
        ],
        out_shape=[
            jax.ShapeDtypeStruct((n, D_MODEL), jnp.bfloat16),
            jax.ShapeDtypeStruct((batch, A_HALO, WIDTH), jnp.float32),
            jax.ShapeDtypeStruct((batch, B_HALO, WIDTH), jnp.float32),
        ],
        scratch_shapes=[
            pltpu.VMEM((A_HALO + tb, WIDTH), jnp.float32),
            pltpu.VMEM((B_HALO + tb, WIDTH), jnp.float32),
            pltpu.VMEM((tb, WIDTH), jnp.float32),
            pltpu.VMEM((SUBLANES - 1, A_HALO + tb - SUBLANES, WIDTH), jnp.float32),
        ],
        compiler_params=_params(("arbitrary",)),
        name="conv_prompt",
    )(proj, wa, ba, lg, lb, wb)


def _conv_sample_kernel(p_ref, sa_ref, sb_ref, wa_ref, ba_ref, lg_ref, lb_ref, wb_ref, mix_in,
                        mix_ref, na_ref, nb_ref):
    del mix_in
    ha = CONV_A_W - 1
    hb = CONV_B_W - 1
    a_glu = p_ref[:, 0:WIDTH] * _sigmoid(p_ref[:, WIDTH:2 * WIDTH])
    acc = wa_ref[ha:ha + 1, :] * a_glu
    for j in range(ha):
        acc = acc + wa_ref[j:j + 1, :] * sa_ref[:, j * WIDTH:(j + 1) * WIDTH]
    a = _ln_silu(acc + ba_ref[...], lg_ref[...], lb_ref[...])
    mix_ref[:, 0:WIDTH] = a.astype(mix_ref.dtype)
    na_ref[:, 0:(ha - 1) * WIDTH] = sa_ref[:, WIDTH:ha * WIDTH]
    na_ref[:, (ha - 1) * WIDTH:ha * WIDTH] = a_glu

    cb = p_ref[:, 3 * WIDTH:4 * WIDTH] * p_ref[:, 4 * WIDTH:5 * WIDTH]
    z = wb_ref[hb:hb + 1, :] * cb
    for j in range(hb):
        z = z + wb_ref[j:j + 1, :] * sb_ref[:, j * WIDTH:(j + 1) * WIDTH]
    mix_ref[:, WIDTH:2 * WIDTH] = (p_ref[:, 2 * WIDTH:3 * WIDTH] * z).astype(mix_ref.dtype)
    nb_ref[:, 0:(hb - 1) * WIDTH] = sb_ref[:, WIDTH:hb * WIDTH]
    nb_ref[:, (hb - 1) * WIDTH:hb * WIDTH] = cb


def _conv_sample(proj, sa2, sb2, wa, ba, lg, lb, wb, mix, layer, n_prompt, bc):
    dec = sa2.shape[1]
    ha = CONV_A_W - 1
    hb = CONV_B_W - 1
    r0 = n_prompt // bc
    vec = lambda: pl.BlockSpec((None, 1, WIDTH), lambda i: (layer, 0, 0))
    return pl.pallas_call(
        _conv_sample_kernel,
        grid=(dec // bc,),
        in_specs=[
            pl.BlockSpec((bc, PROJ_IN), lambda i: (r0 + i, 0)),
            pl.BlockSpec((None, bc, ha * WIDTH), lambda i: (layer, i, 0)),
            pl.BlockSpec((None, bc, hb * WIDTH), lambda i: (layer, i, 0)),
            pl.BlockSpec((None, CONV_A_W, WIDTH), lambda i: (layer, 0, 0)),
            vec(), vec(), vec(),
            pl.BlockSpec((None, CONV_B_W, WIDTH), lambda i: (layer, 0, 0)),
            pl.BlockSpec(memory_space=pl.ANY),
        ],
        out_specs=[
            pl.BlockSpec((bc, D_MODEL), lambda i: (r0 + i, 0)),
            pl.BlockSpec((bc, ha * WIDTH), lambda i: (i, 0)),
            pl.BlockSpec((bc, hb * WIDTH), lambda i: (i, 0)),
        ],
        out_shape=[
            jax.ShapeDtypeStruct(mix.shape, mix.dtype),
            jax.ShapeDtypeStruct((dec, ha * WIDTH), jnp.float32),
            jax.ShapeDtypeStruct((dec, hb * WIDTH), jnp.float32),
        ],
        input_output_aliases={8: 0},
        compiler_params=_params(("parallel",)),
        name="conv_sample",
    )(proj, sa2, sb2, wa, ba, lg, lb, wb, mix)


def _matmul_res_kernel(m_ref, w_ref, h_ref, o_ref):
    o_ref[...] = h_ref[...] + jnp.dot(m_ref[...], w_ref[...],
                                      preferred_element_type=jnp.float32)


def _matmul_res(mix, w, h, layer, tm, tn):
    n, k = mix.shape
    d = w.shape[-1]
    return pl.pallas_call(
        _matmul_res_kernel,
        grid=(n // tm, d // tn),
        in_specs=[
            pl.BlockSpec((tm, k), lambda i, j: (i, 0)),
            pl.BlockSpec((None, k, tn), lambda i, j: (layer, 0, j)),
            pl.BlockSpec((tm, tn), lambda i, j: (i, j)),
        ],
        out_specs=pl.BlockSpec((tm, tn), lambda i, j: (i, j)),
        out_shape=jax.ShapeDtypeStruct((n, d), jnp.float32),
        compiler_params=_params(("parallel", "arbitrary")),
        name="matmul_res",
    )(mix, w, h)


def _peer_scores_kernel(h_ref, g_ref, wq_ref, k1_ref, k2_ref, ut_ref, s1_ref, s2_ref, q_scr):
    u32 = _rms(h_ref[...], g_ref[...])
    ut_ref[...] = u32.T.astype(jnp.bfloat16)
    q_scr[...] = jnp.dot(u32.astype(jnp.bfloat16), wq_ref[...],
                         preferred_element_type=jnp.float32)
    k1 = k1_ref[...].astype(jnp.bfloat16)
    k2 = k2_ref[...].astype(jnp.bfloat16)
    for h in range(PEER_HEADS):
        c0 = h * PEER_QDIM
        q1 = q_scr[:, c0:c0 + HALF].astype(jnp.bfloat16)
        q2 = q_scr[:, c0 + HALF:c0 + PEER_QDIM].astype(jnp.bfloat16)
        s1_ref[h] = lax.dot_general(k1, q1, _NT, preferred_element_type=jnp.float32)
        s2_ref[h] = lax.dot_general(k2, q2, _NT, preferred_element_type=jnp.float32)


def _peer_scores(h, g, wq, k1, k2, layer, tm):
    n, d = h.shape
    qd = wq.shape[-1]
    sshape = jax.ShapeDtypeStruct((PEER_HEADS, N_KEYS, n), jnp.float32)
    sspec = lambda: pl.BlockSpec((PEER_HEADS, N_KEYS, tm), lambda i: (0, 0, i))
    return pl.pallas_call(
        _peer_scores_kernel,
        grid=(n // tm,),
        in_specs=[
            pl.BlockSpec((tm, d), lambda i: (i, 0)),
            pl.BlockSpec((None, 1, d), lambda i: (layer, 0, 0)),
            pl.BlockSpec((None, d, qd), lambda i: (layer, 0, 0)),
            pl.BlockSpec((None, N_KEYS, HALF), lambda i: (layer, 0, 0)),
            pl.BlockSpec((None, N_KEYS, HALF), lambda i: (layer, 0, 0)),
        ],
        out_specs=[pl.BlockSpec((d, tm), lambda i: (0, i)), sspec(), sspec()],
        out_shape=[jax.ShapeDtypeStruct((d, n), jnp.bfloat16), sshape, sshape],
        scratch_shapes=[pltpu.VMEM((tm, qd), jnp.float32)],
        compiler_params=_params(("parallel",)),
        name="peer_scores",
    )(h, g, wq, k1, k2)


def _top_values(x, k):
    out = []
    for _ in range(k):
        m = jnp.max(x, axis=0, keepdims=True)
        out.append(m)
        x = jnp.where(x == m, -jnp.inf, x)
    return out


def _peer_topk_kernel(s1_ref, s2_ref, thk_ref, e1k_ref, e2_ref):
    kk = TOPK + 1
    for h in range(PEER_HEADS):
        s1 = s1_ref[h]
        s2 = s2_ref[h]
        v1 = _top_values(s1, kk)
        v2 = jnp.concatenate(_top_values(s2, kk), axis=0)
        cand = [v1[a] + v2[0:kk // (a + 1), :] for a in range(kk)]
        n_cand = sum(c.shape[0] for c in cand)
        pad = -n_cand % SUBLANES
        if pad:
            cand.append(jnp.full((pad, LANES), -jnp.inf, jnp.float32))
        sc = _top_values(jnp.concatenate(cand, axis=0), kk)
        top = sc[0]
        z = jnp.zeros_like(top)
        for c in sc[:TOPK]:
            z = z + jnp.exp(c - top)
        tau = 0.5 * (sc[TOPK - 1] + sc[TOPK])
        thk_ref[:, h, :] = tau - s1
        e1k_ref[:, h, :] = jnp.exp(s1 - v1[0]) * (1.0 / z)
        e2_ref[h] = jnp.exp(s2 - v2[0:1, :])


def _peer_topk(s1, s2):
    n = s1.shape[-1]
    sspec = lambda: pl.BlockSpec((PEER_HEADS, N_KEYS, LANES), lambda i: (0, 0, i))
    kspec = lambda: pl.BlockSpec((N_KEYS, PEER_HEADS, LANES), lambda i: (0, 0, i))
    kshape = jax.ShapeDtypeStruct((N_KEYS, PEER_HEADS, n), jnp.float32)
    return pl.pallas_call(
        _peer_topk_kernel,
        grid=(n // LANES,),
        in_specs=[sspec(), sspec()],
        out_specs=[kspec(), kspec(), sspec()],
        out_shape=[kshape, kshape, jax.ShapeDtypeStruct(s1.shape, jnp.float32)],
        compiler_params=_params(("parallel",)),
        name="peer_topk",
    )(s1, s2)


def _peer_kernel(ut_ref, th_ref, e1_ref, s2_ref, e2_ref, eu_ref, evt_ref, o_ref, a_scr, w_scr, acc,
                 *, tm, nb):
    j = pl.program_id(1)
    n_i1 = a_scr.shape[0] // N_KEYS

    @pl.when(j == 0)
    def _():
        acc[...] = jnp.zeros_like(acc)

    a_scr[...] = jnp.dot(eu_ref[...], ut_ref[...], preferred_element_type=jnp.float32)
    for lg in range(tm // LANES):
        ls = slice(lg * LANES, (lg + 1) * LANES)
        for il in range(n_i1):
            er = slice(il * N_KEYS, (il + 1) * N_KEYS)
            g = jnp.zeros((N_KEYS, LANES), jnp.float32)
            for h in range(PEER_HEADS):
                sel = s2_ref[h, :, ls] >= th_ref[il, h:h + 1, ls]
                g = g + jnp.where(sel, e2_ref[h, :, ls] * e1_ref[il, h:h + 1, ls], 0.0)
            a = a_scr[er, ls]
            act = 0.5 * a * (1.0 + lax.erf(a * math.sqrt(0.5)))
            w_scr[er, ls] = (g * act).astype(w_scr.dtype)
    acc[...] += jnp.dot(evt_ref[...], w_scr[...], preferred_element_type=jnp.float32)

    @pl.when(j == nb - 1)
    def _():
        o_ref[...] = acc[...].T


def _peer(ut, thk, s2, e1k, e2, eu, evt, layer, tm):
    d, n = ut.shape
    te = SUBLANES * N_KEYS
    nb = N_EXPERTS // te
    once = pl.Buffered(1)
    rows = lambda: pl.BlockSpec((SUBLANES, PEER_HEADS, tm), lambda i, j: (j, 0, i))
    full = lambda: pl.BlockSpec((PEER_HEADS, N_KEYS, tm), lambda i, j: (0, 0, i),
                                pipeline_mode=once)
    return pl.pallas_call(
        functools.partial(_peer_kernel, tm=tm, nb=nb),
        grid=(n // tm, nb),
        in_specs=[
            pl.BlockSpec((d, tm), lambda i, j: (0, i), pipeline_mode=once),
            rows(), rows(), full(), full(),
            pl.BlockSpec((None, te, d), lambda i, j: (layer, j, 0)),
            pl.BlockSpec((None, d, te), lambda i, j: (layer, 0, j)),
        ],
        out_specs=pl.BlockSpec((tm, d), lambda i, j: (i, 0), pipeline_mode=once),
        out_shape=jax.ShapeDtypeStruct((n, d), jnp.float32),
        scratch_shapes=[pltpu.VMEM((te, tm), jnp.float32), pltpu.VMEM((te, tm), jnp.bfloat16),
                        pltpu.VMEM((d, tm), jnp.float32)],
        compiler_params=_params(("parallel", "arbitrary")),
        name="peer_dense",
    )(ut, thk, e1k, s2, e2, eu, evt)


def _ple_kernel(h_ref, dl_ref, p_ref, g_ref, wg_ref, wp_ref, o_ref, h2_scr, u_scr, *, tn):
    j = pl.program_id(1)

    @pl.when(j == 0)
    def _():
        h2 = h_ref[...] + dl_ref[...]
        h2_scr[...] = h2
        u_scr[...] = _rms(h2, g_ref[...]).astype(jnp.bfloat16)

    gate = _sigmoid(jnp.dot(u_scr[...], wg_ref[...], preferred_element_type=jnp.float32))
    emb = jnp.dot(p_ref[...].astype(jnp.bfloat16), wp_ref[...],
                  preferred_element_type=jnp.float32)
    c0 = pl.multiple_of(j * tn, tn)
    o_ref[...] = h2_scr[:, pl.ds(c0, tn)] + emb * gate


def _ple(h, delta, p, g, wg, wp, layer, tm, tn):
    n, d = h.shape
    pd = p.shape[-1]
    return pl.pallas_call(
        functools.partial(_ple_kernel, tn=tn),
        grid=(n // tm, d // tn),
        in_specs=[
            pl.BlockSpec((tm, d), lambda i, j: (i, 0)),
            pl.BlockSpec((tm, d), lambda i, j: (i, 0)),
            pl.BlockSpec((None, tm, pd), lambda i, j: (layer, i, 0)),
            pl.BlockSpec((None, 1, d), lambda i, j: (layer, 0, 0)),
            pl.BlockSpec((None, d, tn), lambda i, j: (layer, 0, j)),
            pl.BlockSpec((None, pd, tn), lambda i, j: (layer, 0, j)),
        ],
        out_specs=pl.BlockSpec((tm, tn), lambda i, j: (i, j)),
        out_shape=jax.ShapeDtypeStruct((n, d), jnp.float32),
        scratch_shapes=[pltpu.VMEM((tm, d), jnp.float32), pltpu.VMEM((tm, d), jnp.bfloat16)],
        compiler_params=_params(("parallel", "arbitrary")),
        name="ple",
    )(h, delta, p, g, wg, wp)


def _final_norm_kernel(h_ref, g_ref, o_ref):
    o_ref[...] = _rms(h_ref[...], g_ref[...])


def _final_norm(h, g, row0, rows):
    d = h.shape[1]
    tm = max(t for t in (4 * LANES, 2 * LANES, LANES) if rows % t == 0 and row0 % t == 0)
    b0 = row0 // tm
    return pl.pallas_call(
        _final_norm_kernel,
        grid=(rows // tm,),
        in_specs=[pl.BlockSpec((tm, d), lambda i: (b0 + i, 0)),
                  pl.BlockSpec((1, d), lambda i: (0, 0))],
        out_specs=pl.BlockSpec((tm, d), lambda i: (i, 0)),
        out_shape=jax.ShapeDtypeStruct((rows, d), jnp.float32),
        compiler_params=_params(("parallel",)),
        name="final_norm",
    )(h, g)


def kernel(x_prompt, x_sample, p_prompt, p_sample, state_conv_a, state_conv_b, norm1_g, w_in, conv_a_w, conv_a_b, ln_a_g, ln_a_b, conv_b_w, w_out, norm2_g, peer_wq, peer_k1, peer_k2, peer_u, peer_v, norm3_g, ple_w, ple_gate_w, final_g):
    batch, seq, d = x_prompt.shape
    dec = x_sample.shape[0]
    depth = w_in.shape[0]
    n_prompt = batch * seq
    n = n_prompt + dec
    assert d == D_MODEL and x_sample.shape[1] == 1
    assert seq % LANES == 0 and dec % LANES == 0 and seq >= A_HALO
    tp = PEER_TOKEN_TILE
    tm = tp // 2
    n_pad = -(-n // tp) * tp
    tb = LANES
    bc = 32
    ha = CONV_A_W - 1
    hb = CONV_B_W - 1
    bf16 = jnp.bfloat16

    pd = p_prompt.shape[-1]
    h = jnp.concatenate([x_prompt.reshape(n_prompt, d), x_sample.reshape(dec, d),
                         jnp.zeros((n_pad - n, d), x_prompt.dtype)], axis=0)
    p_all = jnp.concatenate([p_prompt.reshape(depth, n_prompt, pd),
                             p_sample.reshape(depth, dec, pd),
                             jnp.zeros((depth, n_pad - n, pd), p_prompt.dtype)], axis=1)
    sa2 = state_conv_a.reshape(depth, dec, ha * WIDTH)
    sb2 = state_conv_b.reshape(depth, dec, hb * WIDTH)
    row = lambda v: v.reshape(depth, 1, v.shape[-1])
    w_in_b, w_out_b, wq_b = w_in.astype(bf16), w_out.astype(bf16), peer_wq.astype(bf16)
    eu_b, evt_b = peer_u.astype(bf16), jnp.swapaxes(peer_v, 1, 2).astype(bf16)
    wg_b, wp_b = ple_gate_w.astype(bf16), ple_w.astype(bf16)
    n1, n2, n3 = row(norm1_g), row(norm2_g), row(norm3_g)
    ba, lg, lb = row(conv_a_b), row(ln_a_g), row(ln_a_b)

    na_p, na_s, nb_p, nb_s = [], [], [], []
    for l in range(depth):
        proj = _norm_matmul(h, n1, w_in_b, l, tp, PROJ_IN // 4)
        mix, ta, tbb = _conv_prompt(proj, conv_a_w, ba, lg, lb, conv_b_w, l, batch, seq, tb)
        mix, sa_new, sb_new = _conv_sample(proj, sa2, sb2, conv_a_w, ba, lg, lb, conv_b_w,
                                           mix, l, n_prompt, bc)
        na_p.append(ta[:, A_HALO - ha:, :])
        nb_p.append(tbb[:, B_HALO - hb:, :])
        na_s.append(sa_new.reshape(dec, ha, WIDTH))
        nb_s.append(sb_new.reshape(dec, hb, WIDTH))
        h1 = _matmul_res(mix, w_out_b, h, l, tp, d // 2)
        ut, s1, s2 = _peer_scores(h1, n2, wq_b, peer_k1, peer_k2, l, tm)
        thk, e1k, e2 = _peer_topk(s1, s2)
        delta = _peer(ut, thk, s2, e1k, e2, eu_b, evt_b, l, tp)
        h = _ple(h1, delta, p_all, n3, wg_b, wp_b, l, tm, d)

    fg = final_g.reshape(1, d)
    y_prompt = _final_norm(h, fg, 0, n_prompt).reshape(batch, seq, d)
    y_sample = _final_norm(h, fg, n_prompt, dec).reshape(dec, 1, d)
    return (y_prompt, y_sample, jnp.stack(na_p), jnp.stack(na_s), jnp.stack(nb_p), jnp.stack(nb_s))
```

```python
import functools
import math

import jax
import jax.numpy as jnp
from jax import lax
from jax.experimental import pallas as pl
from jax.experimental.pallas import tpu as pltpu

D_MODEL = 2048
WIDTH = D_MODEL // 2
CONV_A_W = 31
CONV_B_W = 3
PROJ_IN = 5 * WIDTH
PEER_HEADS = 8
N_KEYS = 128
N_EXPERTS = N_KEYS * N_KEYS
HALF = 128
PEER_QDIM = 2 * HALF
TOPK = 16
RMS_EPS = 1e-6
LN_EPS = 1e-5

LANES = 128
SUBLANES = 8
A_HALO = 32
B_HALO = 8
VMEM_LIMIT = 56 * 1024 * 1024

_NT = (((1,), (1,)), ((), ()))
_TN = (((0,), (0,)), ((), ()))


def _params(sem):
    return pltpu.CompilerParams(dimension_semantics=sem, vmem_limit_bytes=VMEM_LIMIT)


MXU_TILE = 256
PEER_TOKEN_TILE = 3 * MXU_TILE


def _rms(x, g):
    ms = jnp.mean(x * x, axis=-1, keepdims=True)
    return x * lax.rsqrt(ms + RMS_EPS) * g


def _sigmoid(x):
    return 1.0 / (1.0 + jnp.exp(-x))


def _norm_matmul_kernel(x_ref, g_ref, w_ref, o_ref, u_scr):
    @pl.when(pl.program_id(1) == 0)
    def _():
        u_scr[...] = _rms(x_ref[...], g_ref[...]).astype(jnp.bfloat16)

    o_ref[...] = jnp.dot(u_scr[...], w_ref[...], preferred_element_type=jnp.float32)


def _norm_matmul(x, g, w, layer, tm, tn):
    n, d = x.shape
    e = w.shape[-1]
    return pl.pallas_call(
        _norm_matmul_kernel,
        grid=(n // tm, e // tn),
        in_specs=[
            pl.BlockSpec((tm, d), lambda i, j: (i, 0)),
            pl.BlockSpec((None, 1, d), lambda i, j: (layer, 0, 0)),
            pl.BlockSpec((None, d, tn), lambda i, j: (layer, 0, j)),
        ],
        out_specs=pl.BlockSpec((tm, tn), lambda i, j: (i, j)),
        out_shape=jax.ShapeDtypeStruct((n, e), jnp.float32),
        scratch_shapes=[pltpu.VMEM((tm, d), jnp.bfloat16)],
        compiler_params=_params(("parallel", "arbitrary")),
        name="norm_matmul",
    )(x, g, w)


def _ln_silu(y, g, b):
    mu = jnp.mean(y, axis=-1, keepdims=True)
    yc = y - mu
    var = jnp.mean(yc * yc, axis=-1, keepdims=True)
    ln = yc * lax.rsqrt(var + LN_EPS) * g + b
    return ln * _sigmoid(ln)


def _conv_prompt_kernel(p_ref, wa_ref, ba_ref, lg_ref, lb_ref, wb_ref,
                        mix_ref, na_ref, nb_ref, abuf, bbuf, ybuf, ashift, *, tb, cw, nt, n_real):
    s = pl.program_id(0)

    @pl.when(s >= n_real)
    def _():
        mix_ref[...] = jnp.zeros_like(mix_ref)

    @pl.when(s < n_real)
    def _():
        _conv_prompt_block(p_ref, wa_ref, ba_ref, lg_ref, lb_ref, wb_ref, mix_ref, na_ref, nb_ref,
                           abuf, bbuf, ybuf, ashift, s % nt, tb, cw)


def _conv_prompt_block(p_ref, wa_ref, ba_ref, lg_ref, lb_ref, wb_ref, mix_ref, na_ref, nb_ref,
                       abuf, bbuf, ybuf, ashift, t, tb, cw):
    @pl.when(t == 0)
    def _():
        abuf[0:A_HALO, :] = jnp.zeros((A_HALO, WIDTH), jnp.float32)
        bbuf[0:B_HALO, :] = jnp.zeros((B_HALO, WIDTH), jnp.float32)

    @pl.when(t > 0)
    def _():
        abuf[0:A_HALO, :] = abuf[tb:tb + A_HALO, :]
        bbuf[0:B_HALO, :] = bbuf[tb:tb + B_HALO, :]

    a_glu = p_ref[:, 0:WIDTH] * _sigmoid(p_ref[:, WIDTH:2 * WIDTH])
    abuf[A_HALO:A_HALO + tb, :] = a_glu
    n_sh = ashift.shape[1]
    for r in range(1, SUBLANES):
        ashift[r - 1, :, :] = abuf[r:r + n_sh, :]
    off_a = A_HALO - (CONV_A_W - 1)
    for c0 in range(0, WIDTH, cw):
        acc = jnp.zeros((tb, cw), jnp.float32)
        for j in range(CONV_A_W):
            q, r = divmod(off_a + j, SUBLANES)
            q *= SUBLANES
            if r == 0:
                rows = abuf[q:q + tb, c0:c0 + cw]
            else:
                rows = ashift[r - 1, q:q + tb, c0:c0 + cw]
            acc = acc + wa_ref[j:j + 1, c0:c0 + cw] * rows
        ybuf[:, c0:c0 + cw] = acc + ba_ref[:, c0:c0 + cw]
    a = _ln_silu(ybuf[...], lg_ref[...], lb_ref[...])
    mix_ref[:, 0:WIDTH] = a.astype(mix_ref.dtype)

    cb = p_ref[:, 3 * WIDTH:4 * WIDTH] * p_ref[:, 4 * WIDTH:5 * WIDTH]
    bbuf[B_HALO:B_HALO + tb, :] = cb
    off_b = B_HALO - (CONV_B_W - 1)
    z = jnp.zeros((tb, WIDTH), jnp.float32)
    for j in range(CONV_B_W):
        z = z + wb_ref[j:j + 1, :] * bbuf[off_b + j:off_b + j + tb, :]
    mix_ref[:, WIDTH:2 * WIDTH] = (p_ref[:, 2 * WIDTH:3 * WIDTH] * z).astype(mix_ref.dtype)

    na_ref[...] = abuf[tb:tb + A_HALO, :]
    nb_ref[...] = bbuf[tb:tb + B_HALO, :]


def _conv_prompt(proj, wa, ba, lg, lb, wb, layer, batch, seq, tb):
    n = proj.shape[0]
    nt = seq // tb
    n_real = batch * nt
    seq_of = lambda s: jnp.minimum(s // nt, batch - 1)
    vec = lambda: pl.BlockSpec((None, 1, WIDTH), lambda s: (layer, 0, 0))
    return pl.pallas_call(
        functools.partial(_conv_prompt_kernel, tb=tb, cw=2 * LANES, nt=nt, n_real=n_real),
        grid=(n // tb,),
        in_specs=[
            pl.BlockSpec((tb, PROJ_IN), lambda s: (s, 0)),
            pl.BlockSpec((None, CONV_A_W, WIDTH), lambda s: (layer, 0, 0)),
            vec(), vec(), vec(),
            pl.BlockSpec((None, CONV_B_W, WIDTH), lambda s: (layer, 0, 0)),
        ],
        out_specs=[
            pl.BlockSpec((tb, D_MODEL), lambda s: (s, 0)),
            pl.BlockSpec((None, A_HALO, WIDTH), lambda s: (seq_of(s), 0, 0)),
            pl.BlockSpec((None, B_HALO, WIDTH), lambda s: (seq_of(s), 0, 0)),
        ],
        out_shape=[
            jax.ShapeDtypeStruct((n, D_MODEL), jnp.bfloat16),
            jax.ShapeDtypeStruct((batch, A_HALO, WIDTH), jnp.float32),
            jax.ShapeDtypeStruct((batch, B_HALO, WIDTH), jnp.float32),
        ],
        scratch_shapes=[
            pltpu.VMEM((A_HALO + tb, WIDTH), jnp.float32),
            pltpu.VMEM((B_HALO + tb, WIDTH), jnp.float32),
            pltpu.VMEM((tb, WIDTH), jnp.float32),
            pltpu.VMEM((SUBLANES - 1, A_HALO + tb - SUBLANES, WIDTH), jnp.float32),
        ],
        compiler_params=_params(("arbitrary",)),
        name="conv_prompt",
    )(proj, wa, ba, lg, lb, wb)


def _conv_sample_kernel(p_ref, sa_ref, sb_ref, wa_ref, ba_ref, lg_ref, lb_ref, wb_ref, mix_in,
                        mix_ref, na_ref, nb_ref):
    del mix_in
    ha = CONV_A_W - 1
    hb = CONV_B_W - 1
    a_glu = p_ref[:, 0:WIDTH] * _sigmoid(p_ref[:, WIDTH:2 * WIDTH])
    acc = wa_ref[ha:ha + 1, :] * a_glu
    for j in range(ha):
        acc = acc + wa_ref[j:j + 1, :] * sa_ref[:, j * WIDTH:(j + 1) * WIDTH]
    a = _ln_silu(acc + ba_ref[...], lg_ref[...], lb_ref[...])
    mix_ref[:, 0:WIDTH] = a.astype(mix_ref.dtype)
    na_ref[:, 0:(ha - 1) * WIDTH] = sa_ref[:, WIDTH:ha * WIDTH]
    na_ref[:, (ha - 1) * WIDTH:ha * WIDTH] = a_glu

    cb = p_ref[:, 3 * WIDTH:4 * WIDTH] * p_ref[:, 4 * WIDTH:5 * WIDTH]
    z = wb_ref[hb:hb + 1, :] * cb
    for j in range(hb):
        z = z + wb_ref[j:j + 1, :] * sb_ref[:, j * WIDTH:(j + 1) * WIDTH]
    mix_ref[:, WIDTH:2 * WIDTH] = (p_ref[:, 2 * WIDTH:3 * WIDTH] * z).astype(mix_ref.dtype)
    nb_ref[:, 0:(hb - 1) * WIDTH] = sb_ref[:, WIDTH:hb * WIDTH]
    nb_ref[:, (hb - 1) * WIDTH:hb * WIDTH] = cb


def _conv_sample(proj, sa2, sb2, wa, ba, lg, lb, wb, mix, layer, n_prompt, bc):
    dec = sa2.shape[1]
    ha = CONV_A_W - 1
    hb = CONV_B_W - 1
    r0 = n_prompt // bc
    vec = lambda: pl.BlockSpec((None, 1, WIDTH), lambda i: (layer, 0, 0))
    return pl.pallas_call(
        _conv_sample_kernel,
        grid=(dec // bc,),
        in_specs=[
            pl.BlockSpec((bc, PROJ_IN), lambda i: (r0 + i, 0)),
            pl.BlockSpec((None, bc, ha * WIDTH), lambda i: (layer, i, 0)),
            pl.BlockSpec((None, bc, hb * WIDTH), lambda i: (layer, i, 0)),
            pl.BlockSpec((None, CONV_A_W, WIDTH), lambda i: (layer, 0, 0)),
            vec(), vec(), vec(),
            pl.BlockSpec((None, CONV_B_W, WIDTH), lambda i: (layer, 0, 0)),
            pl.BlockSpec(memory_space=pl.ANY),
        ],
        out_specs=[
            pl.BlockSpec((bc, D_MODEL), lambda i: (r0 + i, 0)),
            pl.BlockSpec((bc, ha * WIDTH), lambda i: (i, 0)),
            pl.BlockSpec((bc, hb * WIDTH), lambda i: (i, 0)),
        ],
        out_shape=[
            jax.ShapeDtypeStruct(mix.shape, mix.dtype),
            jax.ShapeDtypeStruct((dec, ha * WIDTH), jnp.float32),
            jax.ShapeDtypeStruct((dec, hb * WIDTH), jnp.float32),
        ],
        input_output_aliases={8: 0},
        compiler_params=_params(("parallel",)),
        name="conv_sample",
    )(proj, sa2, sb2, wa, ba, lg, lb, wb, mix)


def _matmul_res_kernel(m_ref, w_ref, h_ref, o_ref):
    o_ref[...] = h_ref[...] + jnp.dot(m_ref[...], w_ref[...],
                                      preferred_element_type=jnp.float32)


def _matmul_res(mix, w, h, layer, tm, tn):
    n, k = mix.shape
    d = w.shape[-1]
    return pl.pallas_call(
        _matmul_res_kernel,
        grid=(n // tm, d // tn),
        in_specs=[
            pl.BlockSpec((tm, k), lambda i, j: (i, 0)),
            pl.BlockSpec((None, k, tn), lambda i, j: (layer, 0, j)),
            pl.BlockSpec((tm, tn), lambda i, j: (i, j)),
        ],
        out_specs=pl.BlockSpec((tm, tn), lambda i, j: (i, j)),
        out_shape=jax.ShapeDtypeStruct((n, d), jnp.float32),
        compiler_params=_params(("parallel", "arbitrary")),
        name="matmul_res",
    )(mix, w, h)


def _peer_scores_kernel(h_ref, g_ref, wq_ref, k1_ref, k2_ref, ut_ref, s1_ref, s2_ref, q_scr):
    u32 = _rms(h_ref[...], g_ref[...])
    ut_ref[...] = u32.T.astype(jnp.bfloat16)
    q_scr[...] = jnp.dot(u32.astype(jnp.bfloat16), wq_ref[...],
                         preferred_element_type=jnp.float32)
    k1 = k1_ref[...].astype(jnp.bfloat16)
    k2 = k2_ref[...].astype(jnp.bfloat16)
    for h in range(PEER_HEADS):
        c0 = h * PEER_QDIM
        q1 = q_scr[:, c0:c0 + HALF].astype(jnp.bfloat16)
        q2 = q_scr[:, c0 + HALF:c0 + PEER_QDIM].astype(jnp.bfloat16)
        s1_ref[h] = lax.dot_general(k1, q1, _NT, preferred_element_type=jnp.float32)
        s2_ref[h] = lax.dot_general(k2, q2, _NT, preferred_element_type=jnp.float32)


def _peer_scores(h, g, wq, k1, k2, layer, tm):
    n, d = h.shape
    qd = wq.shape[-1]
    sshape = jax.ShapeDtypeStruct((PEER_HEADS, N_KEYS, n), jnp.float32)
    sspec = lambda: pl.BlockSpec((PEER_HEADS, N_KEYS, tm), lambda i: (0, 0, i))
    return pl.pallas_call(
        _peer_scores_kernel,
        grid=(n // tm,),
        in_specs=[
            pl.BlockSpec((tm, d), lambda i: (i, 0)),
            pl.BlockSpec((None, 1, d), lambda i: (layer, 0, 0)),
            pl.BlockSpec((None, d, qd), lambda i: (layer, 0, 0)),
            pl.BlockSpec((None, N_KEYS, HALF), lambda i: (layer, 0, 0)),
            pl.BlockSpec((None, N_KEYS, HALF), lambda i: (layer, 0, 0)),
        ],
        out_specs=[pl.BlockSpec((d, tm), lambda i: (0, i)), sspec(), sspec()],
        out_shape=[jax.ShapeDtypeStruct((d, n), jnp.bfloat16), sshape, sshape],
        scratch_shapes=[pltpu.VMEM((tm, qd), jnp.float32)],
        compiler_params=_params(("parallel",)),
        name="peer_scores",
    )(h, g, wq, k1, k2)


def _top_values(x, k):
    out = []
    for _ in range(k):
        m = jnp.max(x, axis=0, keepdims=True)
        out.append(m)
        x = jnp.where(x == m, -jnp.inf, x)
    return out


def _sorting_network(n):
    pairs = []
    p = 1
    while p < n:
        k = p
        while k >= 1:
            for j in range(k % p, n - k, 2 * k):
                for i in range(min(k, n - j - k)):
                    if (i + j) // (2 * p) == (i + j + k) // (2 * p):
                        pairs.append((i + j, i + j + k))
            k //= 2
        p *= 2
    return pairs


def _top_values_tiled(x, k):
    n_t = x.shape[0] // SUBLANES
    t = [x[i * SUBLANES:(i + 1) * SUBLANES, :] for i in range(n_t)]
    for i, j in _sorting_network(n_t):
        t[i], t[j] = jnp.maximum(t[i], t[j]), jnp.minimum(t[i], t[j])
    out = []
    for r in range(k):
        m = jnp.max(t[0], axis=0, keepdims=True)
        out.append(m)
        if r == k - 1:
            break
        hit = t[0] == m
        live = min(n_t, k - r)
        for i in range(live - 1):
            t[i] = jnp.where(hit, t[i + 1], t[i])
        if live == n_t:
            t[n_t - 1] = jnp.where(hit, -jnp.inf, t[n_t - 1])
    return out


def _peer_topk_kernel(s1_ref, s2_ref, thk_ref, e1k_ref, e2_ref):
    kk = TOPK + 1
    for h in range(PEER_HEADS):
        s1 = s1_ref[h]
        s2 = s2_ref[h]
        v1 = _top_values_tiled(s1, kk)
        v2 = jnp.concatenate(_top_values_tiled(s2, kk), axis=0)
        cand = [v1[a] + v2[0:kk // (a + 1), :] for a in range(kk)]
        n_cand = sum(c.shape[0] for c in cand)
        n_rows = SUBLANES * pl.next_power_of_2(pl.cdiv(n_cand, SUBLANES))
        cand.append(jnp.full((n_rows - n_cand, LANES), -jnp.inf, jnp.float32))
        sc = _top_values_tiled(jnp.concatenate(cand, axis=0), kk)
        top = sc[0]
        z = jnp.zeros_like(top)
        for c in sc[:TOPK]:
            z = z + jnp.exp(c - top)
        tau = 0.5 * (sc[TOPK - 1] + sc[TOPK])
        thk_ref[:, h, :] = tau - s1
        e1k_ref[:, h, :] = jnp.exp(s1 - v1[0]) * (1.0 / z)
        e2_ref[h] = jnp.exp(s2 - v2[0:1, :])


def _peer_topk(s1, s2):
    n = s1.shape[-1]
    sspec = lambda: pl.BlockSpec((PEER_HEADS, N_KEYS, LANES), lambda i: (0, 0, i))
    kspec = lambda: pl.BlockSpec((N_KEYS, PEER_HEADS, LANES), lambda i: (0, 0, i))
    kshape = jax.ShapeDtypeStruct((N_KEYS, PEER_HEADS, n), jnp.float32)
    return pl.pallas_call(
        _peer_topk_kernel,
        grid=(n // LANES,),
        in_specs=[sspec(), sspec()],
        out_specs=[kspec(), kspec(), sspec()],
        out_shape=[kshape, kshape, jax.ShapeDtypeStruct(s1.shape, jnp.float32)],
        compiler_params=_params(("parallel",)),
        name="peer_topk",
    )(s1, s2)


def _peer_kernel(ut_ref, th_ref, e1_ref, s2_ref, e2_ref, eu_ref, evt_ref, o_ref, a_scr, w_scr, acc,
                 *, tm, nb):
    j = pl.program_id(1)
    n_i1 = a_scr.shape[0] // N_KEYS

    @pl.when(j == 0)
    def _():
        acc[...] = jnp.zeros_like(acc)

    a_scr[...] = jnp.dot(eu_ref[...], ut_ref[...], preferred_element_type=jnp.float32)
    for lg in range(tm // LANES):
        ls = slice(lg * LANES, (lg + 1) * LANES)
        for il in range(n_i1):
            er = slice(il * N_KEYS, (il + 1) * N_KEYS)
            g = jnp.zeros((N_KEYS, LANES), jnp.float32)
            for h in range(PEER_HEADS):
                sel = s2_ref[h, :, ls] >= th_ref[il, h:h + 1, ls]
                g = g + jnp.where(sel, e2_ref[h, :, ls] * e1_ref[il, h:h + 1, ls], 0.0)
            a = a_scr[er, ls]
            act = 0.5 * a * (1.0 + lax.erf(a * math.sqrt(0.5)))
            w_scr[er, ls] = (g * act).astype(w_scr.dtype)
    acc[...] += jnp.dot(evt_ref[...], w_scr[...], preferred_element_type=jnp.float32)

    @pl.when(j == nb - 1)
    def _():
        o_ref[...] = acc[...].T


def _peer(ut, thk, s2, e1k, e2, eu, evt, layer, tm):
    d, n = ut.shape
    te = SUBLANES * N_KEYS
    nb = N_EXPERTS // te
    once = pl.Buffered(1)
    rows = lambda: pl.BlockSpec((SUBLANES, PEER_HEADS, tm), lambda i, j: (j, 0, i))
    full = lambda: pl.BlockSpec((PEER_HEADS, N_KEYS, tm), lambda i, j: (0, 0, i))
    return pl.pallas_call(
        functools.partial(_peer_kernel, tm=tm, nb=nb),
        grid=(n // tm, nb),
        in_specs=[
            pl.BlockSpec((d, tm), lambda i, j: (0, i)),
            rows(), rows(), full(), full(),
            pl.BlockSpec((None, te, d), lambda i, j: (layer, j, 0)),
            pl.BlockSpec((None, d, te), lambda i, j: (layer, 0, j)),
        ],
        out_specs=pl.BlockSpec((tm, d), lambda i, j: (i, 0), pipeline_mode=once),
        out_shape=jax.ShapeDtypeStruct((n, d), jnp.float32),
        scratch_shapes=[pltpu.VMEM((te, tm), jnp.float32), pltpu.VMEM((te, tm), jnp.bfloat16),
                        pltpu.VMEM((d, tm), jnp.float32)],
        compiler_params=_params(("parallel", "arbitrary")),
        name="peer_dense",
    )(ut, thk, e1k, s2, e2, eu, evt)


def _ple_kernel(h_ref, dl_ref, p_ref, g_ref, wg_ref, wp_ref, o_ref, h2_scr, u_scr, *, tn):
    j = pl.program_id(1)

    @pl.when(j == 0)
    def _():
        h2 = h_ref[...] + dl_ref[...]
        h2_scr[...] = h2
        u_scr[...] = _rms(h2, g_ref[...]).astype(jnp.bfloat16)

    gate = _sigmoid(jnp.dot(u_scr[...], wg_ref[...], preferred_element_type=jnp.float32))
    emb = jnp.dot(p_ref[...].astype(jnp.bfloat16), wp_ref[...],
                  preferred_element_type=jnp.float32)
    c0 = pl.multiple_of(j * tn, tn)
    o_ref[...] = h2_scr[:, pl.ds(c0, tn)] + emb * gate


def _ple(h, delta, p, g, wg, wp, layer, tm, tn):
    n, d = h.shape
    pd = p.shape[-1]
    return pl.pallas_call(
        functools.partial(_ple_kernel, tn=tn),
        grid=(n // tm, d // tn),
        in_specs=[
            pl.BlockSpec((tm, d), lambda i, j: (i, 0)),
            pl.BlockSpec((tm, d), lambda i, j: (i, 0)),
            pl.BlockSpec((None, tm, pd), lambda i, j: (layer, i, 0)),
            pl.BlockSpec((None, 1, d), lambda i, j: (layer, 0, 0)),
            pl.BlockSpec((None, d, tn), lambda i, j: (layer, 0, j)),
            pl.BlockSpec((None, pd, tn), lambda i, j: (layer, 0, j)),
        ],
        out_specs=pl.BlockSpec((tm, tn), lambda i, j: (i, j)),
        out_shape=jax.ShapeDtypeStruct((n, d), jnp.float32),
        scratch_shapes=[pltpu.VMEM((tm, d), jnp.float32), pltpu.VMEM((tm, d), jnp.bfloat16)],
        compiler_params=_params(("parallel", "arbitrary")),
        name="ple",
    )(h, delta, p, g, wg, wp)


def _final_norm_kernel(h_ref, g_ref, o_ref):
    o_ref[...] = _rms(h_ref[...], g_ref[...])


def _final_norm(h, g, row0, rows):
    d = h.shape[1]
    tm = max(t for t in (4 * LANES, 2 * LANES, LANES) if rows % t == 0 and row0 % t == 0)
    b0 = row0 // tm
    return pl.pallas_call(
        _final_norm_kernel,
        grid=(rows // tm,),
        in_specs=[pl.BlockSpec((tm, d), lambda i: (b0 + i, 0)),
                  pl.BlockSpec((1, d), lambda i: (0, 0))],
        out_specs=pl.BlockSpec((tm, d), lambda i: (i, 0)),
        out_shape=jax.ShapeDtypeStruct((rows, d), jnp.float32),
        compiler_params=_params(("parallel",)),
        name="final_norm",
    )(h, g)


def kernel(x_prompt, x_sample, p_prompt, p_sample, state_conv_a, state_conv_b, norm1_g, w_in, conv_a_w, conv_a_b, ln_a_g, ln_a_b, conv_b_w, w_out, norm2_g, peer_wq, peer_k1, peer_k2, peer_u, peer_v, norm3_g, ple_w, ple_gate_w, final_g):
    batch, seq, d = x_prompt.shape
    dec = x_sample.shape[0]
    depth = w_in.shape[0]
    n_prompt = batch * seq
    n = n_prompt + dec
    assert d == D_MODEL and x_sample.shape[1] == 1
    assert seq % LANES == 0 and dec % LANES == 0 and seq >= A_HALO
    tp = PEER_TOKEN_TILE
    tm = tp // 2
    n_pad = -(-n // tp) * tp
    tb = LANES
    bc = 32
    ha = CONV_A_W - 1
    hb = CONV_B_W - 1
    bf16 = jnp.bfloat16

    pd = p_prompt.shape[-1]
    h = jnp.concatenate([x_prompt.reshape(n_prompt, d), x_sample.reshape(dec, d),
                         jnp.zeros((n_pad - n, d), x_prompt.dtype)], axis=0)
    p_all = jnp.concatenate([p_prompt.reshape(depth, n_prompt, pd),
                             p_sample.reshape(depth, dec, pd),
                             jnp.zeros((depth, n_pad - n, pd), p_prompt.dtype)], axis=1)
    sa2 = state_conv_a.reshape(depth, dec, ha * WIDTH)
    sb2 = state_conv_b.reshape(depth, dec, hb * WIDTH)
    row = lambda v: v.reshape(depth, 1, v.shape[-1])
    w_in_b, w_out_b, wq_b = w_in.astype(bf16), w_out.astype(bf16), peer_wq.astype(bf16)
    eu_b, evt_b = peer_u.astype(bf16), jnp.swapaxes(peer_v, 1, 2).astype(bf16)
    wg_b, wp_b = ple_gate_w.astype(bf16), ple_w.astype(bf16)
    n1, n2, n3 = row(norm1_g), row(norm2_g), row(norm3_g)
    ba, lg, lb = row(conv_a_b), row(ln_a_g), row(ln_a_b)

    na_p, na_s, nb_p, nb_s = [], [], [], []
    for l in range(depth):
        proj = _norm_matmul(h, n1, w_in_b, l, tp, PROJ_IN // 4)
        mix, ta, tbb = _conv_prompt(proj, conv_a_w, ba, lg, lb, conv_b_w, l, batch, seq, tb)
        mix, sa_new, sb_new = _conv_sample(proj, sa2, sb2, conv_a_w, ba, lg, lb, conv_b_w,
                                           mix, l, n_prompt, bc)
        na_p.append(ta[:, A_HALO - ha:, :])
        nb_p.append(tbb[:, B_HALO - hb:, :])
        na_s.append(sa_new.reshape(dec, ha, WIDTH))
        nb_s.append(sb_new.reshape(dec, hb, WIDTH))
        h1 = _matmul_res(mix, w_out_b, h, l, tp, d // 2)
        ut, s1, s2 = _peer_scores(h1, n2, wq_b, peer_k1, peer_k2, l, tm)
        thk, e1k, e2 = _peer_topk(s1, s2)
        delta = _peer(ut, thk, s2, e1k, e2, eu_b, evt_b, l, tp)
        h = _ple(h1, delta, p_all, n3, wg_b, wp_b, l, tm, d)

    fg = final_g.reshape(1, d)
    y_prompt = _final_norm(h, fg, 0, n_prompt).reshape(batch, seq, d)
    y_sample = _final_norm(h, fg, n_prompt, dec).reshape(dec, 1, d)
    return (y_prompt, y_sample, jnp.stack(na_p), jnp.stack(na_s), jnp.stack(nb_p), jnp.stack(nb_s))
```

```python
import functools
import math

import jax
import jax.numpy as jnp
from jax import lax
from jax.experimental import pallas as pl
from jax.experimental.pallas import tpu as pltpu

D_MODEL = 2048
WIDTH = D_MODEL // 2
CONV_A_W = 31
CONV_B_W = 3
PROJ_IN = 5 * WIDTH
PEER_HEADS = 8
N_KEYS = 128
N_EXPERTS = N_KEYS * N_KEYS
HALF = 128
PEER_QDIM = 2 * HALF
TOPK = 16
RMS_EPS = 1e-6
LN_EPS = 1e-5

LANES = 128
SUBLANES = 8
A_HALO = 32
B_HALO = 8
VMEM_LIMIT = 56 * 1024 * 1024

_NT = (((1,), (1,)), ((), ()))
_TN = (((0,), (0,)), ((), ()))


def _params(sem):
    return pltpu.CompilerParams(dimension_semantics=sem, vmem_limit_bytes=VMEM_LIMIT)


MXU_TILE = 256
PEER_TOKEN_TILE = 3 * MXU_TILE


def _rms(x, g):
    ms = jnp.mean(x * x, axis=-1, keepdims=True)
    return x * lax.rsqrt(ms + RMS_EPS) * g


def _sigmoid(x):
    return 1.0 / (1.0 + jnp.exp(-x))


def _norm_matmul_kernel(x_ref, g_ref, w_ref, o_ref, u_scr):
    @pl.when(pl.program_id(1) == 0)
    def _():
        u_scr[...] = _rms(x_ref[...], g_ref[...]).astype(jnp.bfloat16)

    o_ref[...] = jnp.dot(u_scr[...], w_ref[...], preferred_element_type=jnp.float32)


def _norm_matmul(x, g, w, layer, tm, tn):
    n, d = x.shape
    e = w.shape[-1]
    return pl.pallas_call(
        _norm_matmul_kernel,
        grid=(n // tm, e // tn),
        in_specs=[
            pl.BlockSpec((tm, d), lambda i, j: (i, 0)),
            pl.BlockSpec((None, 1, d), lambda i, j: (layer, 0, 0)),
            pl.BlockSpec((None, d, tn), lambda i, j: (layer, 0, j)),
        ],
        out_specs=pl.BlockSpec((tm, tn), lambda i, j: (i, j)),
        out_shape=jax.ShapeDtypeStruct((n, e), jnp.float32),
        scratch_shapes=[pltpu.VMEM((tm, d), jnp.bfloat16)],
        compiler_params=_params(("parallel", "arbitrary")),
        name="norm_matmul",
    )(x, g, w)


def _ln_silu(y, g, b):
    mu = jnp.mean(y, axis=-1, keepdims=True)
    yc = y - mu
    var = jnp.mean(yc * yc, axis=-1, keepdims=True)
    ln = yc * lax.rsqrt(var + LN_EPS) * g + b
    return ln * _sigmoid(ln)


def _conv_prompt_kernel(p_ref, wa_ref, ba_ref, lg_ref, lb_ref, wb_ref,
                        mix_ref, na_ref, nb_ref, abuf, bbuf, ybuf, ashift, *, tb, cw, nt, n_real):
    s = pl.program_id(0)

    @pl.when(s >= n_real)
    def _():
        mix_ref[...] = jnp.zeros_like(mix_ref)

    @pl.when(s < n_real)
    def _():
        _conv_prompt_block(p_ref, wa_ref, ba_ref, lg_ref, lb_ref, wb_ref, mix_ref, na_ref, nb_ref,
                           abuf, bbuf, ybuf, ashift, s % nt, tb, cw)


def _conv_prompt_block(p_ref, wa_ref, ba_ref, lg_ref, lb_ref, wb_ref, mix_ref, na_ref, nb_ref,
                       abuf, bbuf, ybuf, ashift, t, tb, cw):
    @pl.when(t == 0)
    def _():
        abuf[0:A_HALO, :] = jnp.zeros((A_HALO, WIDTH), jnp.float32)
        bbuf[0:B_HALO, :] = jnp.zeros((B_HALO, WIDTH), jnp.float32)

    @pl.when(t > 0)
    def _():
        abuf[0:A_HALO, :] = abuf[tb:tb + A_HALO, :]
        bbuf[0:B_HALO, :] = bbuf[tb:tb + B_HALO, :]

    a_glu = p_ref[:, 0:WIDTH] * _sigmoid(p_ref[:, WIDTH:2 * WIDTH])
    abuf[A_HALO:A_HALO + tb, :] = a_glu
    n_sh = ashift.shape[1]
    for r in range(1, SUBLANES):
        ashift[r - 1, :, :] = abuf[r:r + n_sh, :]
    off_a = A_HALO - (CONV_A_W - 1)
    for c0 in range(0, WIDTH, cw):
        acc = jnp.zeros((tb, cw), jnp.float32)
        for j in range(CONV_A_W):
            q, r = divmod(off_a + j, SUBLANES)
            q *= SUBLANES
            if r == 0:
                rows = abuf[q:q + tb, c0:c0 + cw]
            else:
                rows = ashift[r - 1, q:q + tb, c0:c0 + cw]
            acc = acc + wa_ref[j:j + 1, c0:c0 + cw] * rows
        ybuf[:, c0:c0 + cw] = acc + ba_ref[:, c0:c0 + cw]
    a = _ln_silu(ybuf[...], lg_ref[...], lb_ref[...])
    mix_ref[:, 0:WIDTH] = a.astype(mix_ref.dtype)

    cb = p_ref[:, 3 * WIDTH:4 * WIDTH] * p_ref[:, 4 * WIDTH:5 * WIDTH]
    bbuf[B_HALO:B_HALO + tb, :] = cb
    off_b = B_HALO - (CONV_B_W - 1)
    z = jnp.zeros((tb, WIDTH), jnp.float32)
    for j in range(CONV_B_W):
        z = z + wb_ref[j:j + 1, :] * bbuf[off_b + j:off_b + j + tb, :]
    mix_ref[:, WIDTH:2 * WIDTH] = (p_ref[:, 2 * WIDTH:3 * WIDTH] * z).astype(mix_ref.dtype)

    na_ref[...] = abuf[tb:tb + A_HALO, :]
    nb_ref[...] = bbuf[tb:tb + B_HALO, :]


def _conv_prompt(proj, wa, ba, lg, lb, wb, layer, batch, seq, tb):
    n = proj.shape[0]
    nt = seq // tb
    n_real = batch * nt
    seq_of = lambda s: jnp.minimum(s // nt, batch - 1)
    vec = lambda: pl.BlockSpec((None, 1, WIDTH), lambda s: (layer, 0, 0))
    return pl.pallas_call(
        functools.partial(_conv_prompt_kernel, tb=tb, cw=2 * LANES, nt=nt, n_real=n_real),
        grid=(n // tb,),
        in_specs=[
            pl.BlockSpec((tb, PROJ_IN), lambda s: (s, 0)),
            pl.BlockSpec((None, CONV_A_W, WIDTH), lambda s: (layer, 0, 0)),
            vec(), vec(), vec(),
            pl.BlockSpec((None, CONV_B_W, WIDTH), lambda s: (layer, 0, 0)),
        ],
        out_specs=[
            pl.BlockSpec((tb, D_MODEL), lambda s: (s, 0)),
            pl.BlockSpec((None, A_HALO, WIDTH), lambda s: (seq_of(s), 0, 0)),
            pl.BlockSpec((None, B_HALO, WIDTH), lambda s: (seq_of(s), 0, 0)),
        ],
        out_shape=[
            jax.ShapeDtypeStruct((n, D_MODEL), jnp.bfloat16),
            jax.ShapeDtypeStruct((batch, A_HALO, WIDTH), jnp.float32),
            jax.ShapeDtypeStruct((batch, B_HALO, WIDTH), jnp.float32),
        ],
        scratch_shapes=[
            pltpu.VMEM((A_HALO + tb, WIDTH), jnp.float32),
            pltpu.VMEM((B_HALO + tb, WIDTH), jnp.float32),
            pltpu.VMEM((tb, WIDTH), jnp.float32),
            pltpu.VMEM((SUBLANES - 1, A_HALO + tb - SUBLANES, WIDTH), jnp.float32),
        ],
        compiler_params=_params(("arbitrary",)),
        name="conv_prompt",
    )(proj, wa, ba, lg, lb, wb)


def _conv_sample_kernel(p_ref, sa_ref, sb_ref, wa_ref, ba_ref, lg_ref, lb_ref, wb_ref, mix_in,
                        mix_ref, na_ref, nb_ref):
    del mix_in
    ha = CONV_A_W - 1
    hb = CONV_B_W - 1
    a_glu = p_ref[:, 0:WIDTH] * _sigmoid(p_ref[:, WIDTH:2 * WIDTH])
    acc = wa_ref[ha:ha + 1, :] * a_glu
    for j in range(ha):
        acc = acc + wa_ref[j:j + 1, :] * sa_ref[:, j * WIDTH:(j + 1) * WIDTH]
    a = _ln_silu(acc + ba_ref[...], lg_ref[...], lb_ref[...])
    mix_ref[:, 0:WIDTH] = a.astype(mix_ref.dtype)
    na_ref[:, 0:(ha - 1) * WIDTH] = sa_ref[:, WIDTH:ha * WIDTH]
    na_ref[:, (ha - 1) * WIDTH:ha * WIDTH] = a_glu

    cb = p_ref[:, 3 * WIDTH:4 * WIDTH] * p_ref[:, 4 * WIDTH:5 * WIDTH]
    z = wb_ref[hb:hb + 1, :] * cb
    for j in range(hb):
        z = z + wb_ref[j:j + 1, :] * sb_ref[:, j * WIDTH:(j + 1) * WIDTH]
    mix_ref[:, WIDTH:2 * WIDTH] = (p_ref[:, 2 * WIDTH:3 * WIDTH] * z).astype(mix_ref.dtype)
    nb_ref[:, 0:(hb - 1) * WIDTH] = sb_ref[:, WIDTH:hb * WIDTH]
    nb_ref[:, (hb - 1) * WIDTH:hb * WIDTH] = cb


def _conv_sample(proj, sa2, sb2, wa, ba, lg, lb, wb, mix, layer, n_prompt, bc):
    dec = sa2.shape[1]
    ha = CONV_A_W - 1
    hb = CONV_B_W - 1
    r0 = n_prompt // bc
    vec = lambda: pl.BlockSpec((None, 1, WIDTH), lambda i: (layer, 0, 0))
    return pl.pallas_call(
        _conv_sample_kernel,
        grid=(dec // bc,),
        in_specs=[
            pl.BlockSpec((bc, PROJ_IN), lambda i: (r0 + i, 0)),
            pl.BlockSpec((None, bc, ha * WIDTH), lambda i: (layer, i, 0)),
            pl.BlockSpec((None, bc, hb * WIDTH), lambda i: (layer, i, 0)),
            pl.BlockSpec((None, CONV_A_W, WIDTH), lambda i: (layer, 0, 0)),
            vec(), vec(), vec(),
            pl.BlockSpec((None, CONV_B_W, WIDTH), lambda i: (layer, 0, 0)),
            pl.BlockSpec(memory_space=pl.ANY),
        ],
        out_specs=[
            pl.BlockSpec((bc, D_MODEL), lambda i: (r0 + i, 0)),
            pl.BlockSpec((bc, ha * WIDTH), lambda i: (i, 0)),
            pl.BlockSpec((bc, hb * WIDTH), lambda i: (i, 0)),
        ],
        out_shape=[
            jax.ShapeDtypeStruct(mix.shape, mix.dtype),
            jax.ShapeDtypeStruct((dec, ha * WIDTH), jnp.float32),
            jax.ShapeDtypeStruct((dec, hb * WIDTH), jnp.float32),
        ],
        input_output_aliases={8: 0},
        compiler_params=_params(("parallel",)),
        name="conv_sample",
    )(proj, sa2, sb2, wa, ba, lg, lb, wb, mix)


def _matmul_res_kernel(m_ref, w_ref, h_ref, o_ref):
    o_ref[...] = h_ref[...] + jnp.dot(m_ref[...], w_ref[...],
                                      preferred_element_type=jnp.float32)


def _matmul_res(mix, w, h, layer, tm, tn):
    n, k = mix.shape
    d = w.shape[-1]
    return pl.pallas_call(
        _matmul_res_kernel,
        grid=(n // tm, d // tn),
        in_specs=[
            pl.BlockSpec((tm, k), lambda i, j: (i, 0)),
            pl.BlockSpec((None, k, tn), lambda i, j: (layer, 0, j)),
            pl.BlockSpec((tm, tn), lambda i, j: (i, j)),
        ],
        out_specs=pl.BlockSpec((tm, tn), lambda i, j: (i, j)),
        out_shape=jax.ShapeDtypeStruct((n, d), jnp.float32),
        compiler_params=_params(("parallel", "arbitrary")),
        name="matmul_res",
    )(mix, w, h)


def _peer_scores_kernel(h_ref, g_ref, wq_ref, k1_ref, k2_ref, ut_ref, s1_ref, s2_ref, q_scr):
    u32 = _rms(h_ref[...], g_ref[...])
    ut_ref[...] = u32.T.astype(jnp.bfloat16)
    q_scr[...] = jnp.dot(u32.astype(jnp.bfloat16), wq_ref[...],
                         preferred_element_type=jnp.float32)
    k1 = k1_ref[...].astype(jnp.bfloat16)
    k2 = k2_ref[...].astype(jnp.bfloat16)
    for h in range(PEER_HEADS):
        c0 = h * PEER_QDIM
        q1 = q_scr[:, c0:c0 + HALF].astype(jnp.bfloat16)
        q2 = q_scr[:, c0 + HALF:c0 + PEER_QDIM].astype(jnp.bfloat16)
        s1_ref[h] = lax.dot_general(k1, q1, _NT, preferred_element_type=jnp.float32)
        s2_ref[h] = lax.dot_general(k2, q2, _NT, preferred_element_type=jnp.float32)


def _peer_scores(h, g, wq, k1, k2, layer, tm):
    n, d = h.shape
    qd = wq.shape[-1]
    sshape = jax.ShapeDtypeStruct((PEER_HEADS, N_KEYS, n), jnp.float32)
    sspec = lambda: pl.BlockSpec((PEER_HEADS, N_KEYS, tm), lambda i: (0, 0, i))
    return pl.pallas_call(
        _peer_scores_kernel,
        grid=(n // tm,),
        in_specs=[
            pl.BlockSpec((tm, d), lambda i: (i, 0)),
            pl.BlockSpec((None, 1, d), lambda i: (layer, 0, 0)),
            pl.BlockSpec((None, d, qd), lambda i: (layer, 0, 0)),
            pl.BlockSpec((None, N_KEYS, HALF), lambda i: (layer, 0, 0)),
            pl.BlockSpec((None, N_KEYS, HALF), lambda i: (layer, 0, 0)),
        ],
        out_specs=[pl.BlockSpec((d, tm), lambda i: (0, i)), sspec(), sspec()],
        out_shape=[jax.ShapeDtypeStruct((d, n), jnp.bfloat16), sshape, sshape],
        scratch_shapes=[pltpu.VMEM((tm, qd), jnp.float32)],
        compiler_params=_params(("parallel",)),
        name="peer_scores",
    )(h, g, wq, k1, k2)


def _top_values(x, k):
    out = []
    for _ in range(k):
        m = jnp.max(x, axis=0, keepdims=True)
        out.append(m)
        x = jnp.where(x == m, -jnp.inf, x)
    return out


def _sorting_network(n):
    pairs = []
    p = 1
    while p < n:
        k = p
        while k >= 1:
            for j in range(k % p, n - k, 2 * k):
                for i in range(min(k, n - j - k)):
                    if (i + j) // (2 * p) == (i + j + k) // (2 * p):
                        pairs.append((i + j, i + j + k))
            k //= 2
        p *= 2
    return pairs


def _top_values_tiled(x, k):
    n_t = x.shape[0] // SUBLANES
    t = [x[i * SUBLANES:(i + 1) * SUBLANES, :] for i in range(n_t)]
    for i, j in _sorting_network(n_t):
        t[i], t[j] = jnp.maximum(t[i], t[j]), jnp.minimum(t[i], t[j])
    out = []
    for r in range(k):
        m = jnp.max(t[0], axis=0, keepdims=True)
        out.append(m)
        if r == k - 1:
            break
        hit = t[0] == m
        live = min(n_t, k - r)
        for i in range(live - 1):
            t[i] = jnp.where(hit, t[i + 1], t[i])
        if live == n_t:
            t[n_t - 1] = jnp.where(hit, -jnp.inf, t[n_t - 1])
    return out


def _peer_topk_kernel(s1_ref, s2_ref, thk_ref, e1k_ref, e2_ref):
    kk = TOPK + 1
    for h in range(PEER_HEADS):
        s1 = s1_ref[h]
        s2 = s2_ref[h]
        v1 = _top_values_tiled(s1, kk)
        v2 = jnp.concatenate(_top_values_tiled(s2, kk), axis=0)
        cand = [v1[a] + v2[0:kk // (a + 1), :] for a in range(kk)]
        n_cand = sum(c.shape[0] for c in cand)
        n_rows = SUBLANES * pl.next_power_of_2(pl.cdiv(n_cand, SUBLANES))
        cand.append(jnp.full((n_rows - n_cand, LANES), -jnp.inf, jnp.float32))
        sc = _top_values_tiled(jnp.concatenate(cand, axis=0), kk)
        top = sc[0]
        z = jnp.zeros_like(top)
        for c in sc[:TOPK]:
            z = z + jnp.exp(c - top)
        tau = 0.5 * (sc[TOPK - 1] + sc[TOPK])
        thk_ref[:, h, :] = tau - s1
        e1k_ref[:, h, :] = jnp.exp(s1 - v1[0]) * (1.0 / z)
        e2_ref[h] = jnp.exp(s2 - v2[0:1, :])


def _peer_topk(s1, s2):
    n = s1.shape[-1]
    sspec = lambda: pl.BlockSpec((PEER_HEADS, N_KEYS, LANES), lambda i: (0, 0, i))
    kspec = lambda: pl.BlockSpec((N_KEYS, PEER_HEADS, LANES), lambda i: (0, 0, i))
    kshape = jax.ShapeDtypeStruct((N_KEYS, PEER_HEADS, n), jnp.float32)
    return pl.pallas_call(
        _peer_topk_kernel,
        grid=(n // LANES,),
        in_specs=[sspec(), sspec()],
        out_specs=[kspec(), kspec(), sspec()],
        out_shape=[kshape, kshape, jax.ShapeDtypeStruct(s1.shape, jnp.float32)],
        compiler_params=_params(("parallel",)),
        name="peer_topk",
    )(s1, s2)


def _peer_kernel(ut_ref, th_ref, e1_ref, s2_ref, e2_ref, eu_ref, evt_ref, o_ref, a_scr, w_scr, acc,
                 *, tm, nb):
    j = pl.program_id(1)
    n_i1 = a_scr.shape[0] // N_KEYS

    @pl.when(j == 0)
    def _():
        acc[...] = jnp.zeros_like(acc)

    a_scr[...] = jnp.dot(eu_ref[...], ut_ref[...], preferred_element_type=jnp.float32)
    for lg in range(tm // LANES):
        ls = slice(lg * LANES, (lg + 1) * LANES)
        for il in range(n_i1):
            er = slice(il * N_KEYS, (il + 1) * N_KEYS)
            g = jnp.zeros((N_KEYS, LANES), jnp.float32)
            for h in range(PEER_HEADS):
                sel = s2_ref[h, :, ls] >= th_ref[il, h:h + 1, ls]
                g = g + jnp.where(sel, e2_ref[h, :, ls] * e1_ref[il, h:h + 1, ls], 0.0)
            a = a_scr[er, ls]
            act = 0.5 * a * (1.0 + lax.erf(a * math.sqrt(0.5)))
            w_scr[er, ls] = (g * act).astype(w_scr.dtype)
    acc[...] += jnp.dot(evt_ref[...], w_scr[...], preferred_element_type=jnp.float32)

    @pl.when(j == nb - 1)
    def _():
        o_ref[...] = acc[...].T


def _peer(ut, thk, s2, e1k, e2, eu, evt, layer, tm):
    d, n = ut.shape
    te = SUBLANES * N_KEYS
    nb = N_EXPERTS // te
    once = pl.Buffered(1)
    rows = lambda: pl.BlockSpec((SUBLANES, PEER_HEADS, tm), lambda i, j: (j, 0, i))
    full = lambda: pl.BlockSpec((PEER_HEADS, N_KEYS, tm), lambda i, j: (0, 0, i))
    return pl.pallas_call(
        functools.partial(_peer_kernel, tm=tm, nb=nb),
        grid=(n // tm, nb),
        in_specs=[
            pl.BlockSpec((d, tm), lambda i, j: (0, i)),
            rows(), rows(), full(), full(),
            pl.BlockSpec((None, te, d), lambda i, j: (layer, j, 0)),
            pl.BlockSpec((None, d, te), lambda i, j: (layer, 0, j)),
        ],
        out_specs=pl.BlockSpec((tm, d), lambda i, j: (i, 0), pipeline_mode=once),
        out_shape=jax.ShapeDtypeStruct((n, d), jnp.float32),
        scratch_shapes=[pltpu.VMEM((te, tm), jnp.float32), pltpu.VMEM((te, tm), jnp.bfloat16),
                        pltpu.VMEM((d, tm), jnp.float32)],
        compiler_params=_params(("parallel", "arbitrary")),
        name="peer_dense",
    )(ut, thk, e1k, s2, e2, eu, evt)


def _ple_kernel(h_ref, dl_ref, p_ref, g_ref, wg_ref, wp_ref, o_ref, h2_scr, u_scr, *, tn):
    j = pl.program_id(1)

    @pl.when(j == 0)
    def _():
        h2 = h_ref[...] + dl_ref[...]
        h2_scr[...] = h2
        u_scr[...] = _rms(h2, g_ref[...]).astype(jnp.bfloat16)

    gate = _sigmoid(jnp.dot(u_scr[...], wg_ref[...], preferred_element_type=jnp.float32))
    emb = jnp.dot(p_ref[...].astype(jnp.bfloat16), wp_ref[...],
                  preferred_element_type=jnp.float32)
    c0 = pl.multiple_of(j * tn, tn)
    o_ref[...] = h2_scr[:, pl.ds(c0, tn)] + emb * gate


def _ple(h, delta, p, g, wg, wp, layer, tm, tn):
    n, d = h.shape
    pd = p.shape[-1]
    return pl.pallas_call(
        functools.partial(_ple_kernel, tn=tn),
        grid=(n // tm, d // tn),
        in_specs=[
            pl.BlockSpec((tm, d), lambda i, j: (i, 0)),
            pl.BlockSpec((tm, d), lambda i, j: (i, 0)),
            pl.BlockSpec((None, tm, pd), lambda i, j: (layer, i, 0)),
            pl.BlockSpec((None, 1, d), lambda i, j: (layer, 0, 0)),
            pl.BlockSpec((None, d, tn), lambda i, j: (layer, 0, j)),
            pl.BlockSpec((None, pd, tn), lambda i, j: (layer, 0, j)),
        ],
        out_specs=pl.BlockSpec((tm, tn), lambda i, j: (i, j)),
        out_shape=jax.ShapeDtypeStruct((n, d), jnp.float32),
        scratch_shapes=[pltpu.VMEM((tm, d), jnp.float32), pltpu.VMEM((tm, d), jnp.bfloat16)],
        compiler_params=_params(("parallel", "arbitrary")),
        name="ple",
    )(h, delta, p, g, wg, wp)


def _final_norm_kernel(h_ref, g_ref, o_ref):
    o_ref[...] = _rms(h_ref[...], g_ref[...])


def _final_norm(h, g, row0, rows):
    d = h.shape[1]
    tm = max(t for t in (4 * LANES, 2 * LANES, LANES) if rows % t == 0 and row0 % t == 0)
    b0 = row0 // tm
    return pl.pallas_call(
        _final_norm_kernel,
        grid=(rows // tm,),
        in_specs=[pl.BlockSpec((tm, d), lambda i: (b0 + i, 0)),
                  pl.BlockSpec((1, d), lambda i: (0, 0))],
        out_specs=pl.BlockSpec((tm, d), lambda i: (i, 0)),
        out_shape=jax.ShapeDtypeStruct((rows, d), jnp.float32),
        compiler_params=_params(("parallel",)),
        name="final_norm",
    )(h, g)


def kernel(x_prompt, x_sample, p_prompt, p_sample, state_conv_a, state_conv_b, norm1_g, w_in, conv_a_w, conv_a_b, ln_a_g, ln_a_b, conv_b_w, w_out, norm2_g, peer_wq, peer_k1, peer_k2, peer_u, peer_v, norm3_g, ple_w, ple_gate_w, final_g):
    batch, seq, d = x_prompt.shape
    dec = x_sample.shape[0]
    depth = w_in.shape[0]
    n_prompt = batch * seq
    n = n_prompt + dec
    assert d == D_MODEL and x_sample.shape[1] == 1
    assert seq % LANES == 0 and dec % LANES == 0 and seq >= A_HALO
    tp = PEER_TOKEN_TILE
    tm = tp // 2
    n_pad = -(-n // tp) * tp
    tb = LANES
    bc = 32
    ha = CONV_A_W - 1
    hb = CONV_B_W - 1
    bf16 = jnp.bfloat16

    pd = p_prompt.shape[-1]
    h = jnp.concatenate([x_prompt.reshape(n_prompt, d), x_sample.reshape(dec, d),
                         jnp.zeros((n_pad - n, d), x_prompt.dtype)], axis=0)
    p_all = jnp.concatenate([p_prompt.reshape(depth, n_prompt, pd),
                             p_sample.reshape(depth, dec, pd),
                             jnp.zeros((depth, n_pad - n, pd), p_prompt.dtype)], axis=1)
    sa2 = state_conv_a.reshape(depth, dec, ha * WIDTH)
    sb2 = state_conv_b.reshape(depth, dec, hb * WIDTH)
    row = lambda v: v.reshape(depth, 1, v.shape[-1])
    w_in_b, w_out_b, wq_b = w_in.astype(bf16), w_out.astype(bf16), peer_wq.astype(bf16)
    eu_b, evt_b = peer_u.astype(bf16), jnp.swapaxes(peer_v, 1, 2).astype(bf16)
    wg_b, wp_b = ple_gate_w.astype(bf16), ple_w.astype(bf16)
    n1, n2, n3 = row(norm1_g), row(norm2_g), row(norm3_g)
    ba, lg, lb = row(conv_a_b), row(ln_a_g), row(ln_a_b)

    na_p, na_s, nb_p, nb_s = [], [], [], []
    for l in range(depth):
        proj = _norm_matmul(h, n1, w_in_b, l, tp, PROJ_IN // 4)
        mix, ta, tbb = _conv_prompt(proj, conv_a_w, ba, lg, lb, conv_b_w, l, batch, seq, tb)
        mix, sa_new, sb_new = _conv_sample(proj, sa2, sb2, conv_a_w, ba, lg, lb, conv_b_w,
                                           mix, l, n_prompt, bc)
        na_p.append(ta[:, A_HALO - ha:, :])
        nb_p.append(tbb[:, B_HALO - hb:, :])
        na_s.append(sa_new.reshape(dec, ha, WIDTH))
        nb_s.append(sb_new.reshape(dec, hb, WIDTH))
        h1 = _matmul_res(mix, w_out_b, h, l, tp, d)
        ut, s1, s2 = _peer_scores(h1, n2, wq_b, peer_k1, peer_k2, l, tm)
        thk, e1k, e2 = _peer_topk(s1, s2)
        delta = _peer(ut, thk, s2, e1k, e2, eu_b, evt_b, l, tp)
        h = _ple(h1, delta, p_all, n3, wg_b, wp_b, l, tm, d)

    fg = final_g.reshape(1, d)
    y_prompt = _final_norm(h, fg, 0, n_prompt).reshape(batch, seq, d)
    y_sample = _final_norm(h, fg, n_prompt, dec).reshape(dec, 1, d)
    return (y_prompt, y_sample, jnp.stack(na_p), jnp.stack(na_s), jnp.stack(nb_p), jnp.stack(nb_s))
```
